```python
import jax, jax.numpy as jnp
from jax import lax
import numpy as np

D_MODEL = 4096
BATCH = 1
SEQ = 16384
DEPTH = 1
DEC_BATCH = 16
DEC_SEQ = 64
PAST_LEN = 4096

CHUNK = 64
Q_BLOCK = 128
MIX_DIM = D_MODEL // 2
CONV_DIM = MIX_DIM // 2
ATTN_DIM = MIX_DIM - CONV_DIM
HEAD_DIM = 128
N_SB_HEADS = ATTN_DIM // HEAD_DIM
CONV_WIDTH = 31
IN_COLS = 2 * CONV_DIM + 3 * ATTN_DIM
N_GROUPS = 4
EXPERTS_PER_GROUP = 4
TOP_K_IN_GROUP = 2
D_EXPERT = D_MODEL // 32
ALPHA = (2.0 * DEPTH) ** 0.25
BETA_INIT = (8.0 * DEPTH) ** -0.25
SB_SCALE = HEAD_DIM ** -0.5
LN_EPS = 1e-5

kernel_name = "stickbreak_conformer_hiermoe_stream_step"


def layer_norm(x, g=None, b=None):
    xf = x.astype(jnp.float32)
    mu = jnp.mean(xf, axis=-1, keepdims=True)
    xc = xf - mu
    var = jnp.mean(xc * xc, axis=-1, keepdims=True)
    y = xc * lax.rsqrt(var + LN_EPS)
    if g is not None:
        y = y * g.astype(jnp.float32) + b.astype(jnp.float32)
    return y.astype(x.dtype)


def sb_attend(q, k, v, q_pos):
    k_pos = jnp.arange(k.shape[1])
    z = jnp.einsum('bqhd,bkhd->bhqk', q, k).astype(jnp.float32) * SB_SCALE
    mask = k_pos[None, :] < q_pos[:, None]
    log_keep = jnp.where(mask, jax.nn.log_sigmoid(-z), 0.0)
    rc = lax.cumsum(log_keep, axis=3, reverse=True)
    a = jnp.where(mask, jnp.exp(z + rc), 0.0)
    return jnp.einsum('bhqk,bkhd->bqhd', a.astype(v.dtype), v)


def stick_breaking(q, k, v, past):
    B, T, H, D = q.shape
    if T > Q_BLOCK and T % Q_BLOCK == 0:
        outs = []
        for i in range(T // Q_BLOCK):
            end = past + (i + 1) * Q_BLOCK
            q_pos = past + i * Q_BLOCK + jnp.arange(Q_BLOCK)
            outs.append(sb_attend(q[:, i * Q_BLOCK:(i + 1) * Q_BLOCK], k[:, :end], v[:, :end], q_pos))
        return jnp.concatenate(outs, axis=1)
    return sb_attend(q, k, v, past + jnp.arange(T))


def causal_depthwise_conv(a_full, conv_w, conv_b):
    out = lax.conv_general_dilated(a_full, conv_w[:, None, :].astype(a_full.dtype), window_strides=(1,),
                                   padding='VALID', dimension_numbers=('NWC', 'WIO', 'NWC'),
                                   feature_group_count=a_full.shape[-1])
    return out + conv_b.astype(out.dtype)


def hier_moe(h, w_rg, b_rg, w_re, b_re, w_gate, w_up, w_down):
    B, T, _ = h.shape
    lg = (h @ w_rg).astype(jnp.float32) + b_rg.astype(jnp.float32)
    pg = jax.nn.softmax(lg, axis=-1)
    g_idx = jnp.argmax(lg, axis=-1)
    p_top = jnp.max(pg, axis=-1)
    le = ((h @ w_re).astype(jnp.float32) + b_re.astype(jnp.float32)).reshape(B, T, N_GROUPS, EXPERTS_PER_GROUP)
    le_sel = jnp.take_along_axis(le, g_idx[:, :, None, None], axis=2)[:, :, 0]
    top_v, top_i = lax.top_k(le_sel, TOP_K_IN_GROUP)
    w_top = jax.nn.softmax(top_v, axis=-1) * p_top[..., None]
    in_group = jnp.sum(jax.nn.one_hot(top_i, EXPERTS_PER_GROUP, dtype=jnp.float32) * w_top[..., None], axis=2)
    gates = (jax.nn.one_hot(g_idx, N_GROUPS, dtype=jnp.float32)[..., None] * in_group[:, :, None, :]).astype(h.dtype)
    out = jnp.zeros_like(h)
    for g in range(N_GROUPS):
        a = jnp.einsum('btd,edf->btef', h, w_gate[g])
        u = jnp.einsum('btd,edf->btef', h, w_up[g])
        act = jax.nn.silu(a) * u * gates[:, :, g, :, None]
        out = out + jnp.einsum('btef,efd->btd', act, w_down[g])
    return out


def trunk_layer(x, c, conv_prev, k_prev, v_prev, w_ada, b_ada, w_in, conv_w, conv_b, conv_ln_g, conv_ln_b,
                w_out, ln1_g, ln1_b, w_route_group, b_route_group, w_route_expert, b_route_expert,
                w_exp_gate, w_exp_up, w_exp_down, ln2_g, ln2_b):
    B, T, _ = x.shape
    P = k_prev.shape[1]
    mod = (jax.nn.silu(c) @ w_ada + b_ada).reshape(B, 6, 1, D_MODEL)
    shift1, scale1, gate1, shift2, scale2, gate2 = (mod[:, i] for i in range(6))
    h1 = layer_norm(x) * (1.0 + scale1) + shift1
    proj = h1 @ w_in
    cv = proj[..., :CONV_DIM]
    cg = proj[..., CONV_DIM:2 * CONV_DIM]
    q = proj[..., 2 * CONV_DIM:2 * CONV_DIM + ATTN_DIM].reshape(B, T, N_SB_HEADS, HEAD_DIM)
    k = proj[..., 2 * CONV_DIM + ATTN_DIM:2 * CONV_DIM + 2 * ATTN_DIM].reshape(B, T, N_SB_HEADS, HEAD_DIM)
    v = proj[..., 2 * CONV_DIM + 2 * ATTN_DIM:].reshape(B, T, N_SB_HEADS, HEAD_DIM)
    a_glu = cv * jax.nn.sigmoid(cg)
    a_full = jnp.concatenate([conv_prev.astype(a_glu.dtype), a_glu], axis=1)
    conv_out = jax.nn.silu(layer_norm(causal_depthwise_conv(a_full, conv_w, conv_b), conv_ln_g, conv_ln_b))
    k_all = jnp.concatenate([k_prev.astype(k.dtype), k], axis=1)
    v_all = jnp.concatenate([v_prev.astype(v.dtype), v], axis=1)
    attn = stick_breaking(q, k_all, v_all, P).reshape(B, T, ATTN_DIM)
    mix = jnp.concatenate([conv_out, attn], axis=-1) @ w_out
    x1 = layer_norm(ALPHA * x + (1.0 + gate1) * mix, ln1_g, ln1_b)
    h2 = layer_norm(x1) * (1.0 + scale2) + shift2
    ff = hier_moe(h2, w_route_group, b_route_group, w_route_expert, b_route_expert, w_exp_gate, w_exp_up, w_exp_down)
    y = layer_norm(ALPHA * x1 + (1.0 + gate2) * ff, ln2_g, ln2_b)
    return y, a_full[:, -(CONV_WIDTH - 1):], k, v


def setup_inputs(seed: int = 0) -> dict:
    key = jax.random.key(seed)
    ks = jax.random.split(key, 32)
    f32 = jnp.float32

    def nrm(kk, shape, s):
        return jax.random.normal(kk, shape, f32) * s

    L = DEPTH
    G, E, F = N_GROUPS, EXPERTS_PER_GROUP, D_EXPERT
    return {
        "x_prompt": nrm(ks[0], (BATCH, SEQ, D_MODEL), 1.0),
        "x_sample": nrm(ks[1], (DEC_BATCH, DEC_SEQ, D_MODEL), 1.0),
        "cache_k": nrm(ks[2], (L, DEC_BATCH, PAST_LEN, N_SB_HEADS, HEAD_DIM), 1.0),
        "cache_v": nrm(ks[3], (L, DEC_BATCH, PAST_LEN, N_SB_HEADS, HEAD_DIM), 1.0),
        "state_conv": nrm(ks[4], (L, DEC_BATCH, CONV_WIDTH - 1, CONV_DIM), 0.5),
        "c_prompt": nrm(ks[5], (BATCH, D_MODEL), 1.0),
        "c_sample": nrm(ks[6], (DEC_BATCH, D_MODEL), 1.0),
        "w_ada": nrm(ks[7], (L, D_MODEL, 6 * D_MODEL), 0.5 * D_MODEL ** -0.5),
        "b_ada": nrm(ks[8], (L, 6 * D_MODEL), 0.02),
        "w_in": nrm(ks[9], (L, D_MODEL, IN_COLS), D_MODEL ** -0.5),
        "conv_w": nrm(ks[10], (L, CONV_WIDTH, CONV_DIM), CONV_WIDTH ** -0.5),
        "conv_b": nrm(ks[11], (L, CONV_DIM), 0.02),
        "conv_ln_g": 1.0 + nrm(ks[12], (L, CONV_DIM), 0.02),
        "conv_ln_b": nrm(ks[13], (L, CONV_DIM), 0.02),
        "w_out": nrm(ks[14], (L, MIX_DIM, D_MODEL), BETA_INIT * MIX_DIM ** -0.5),
        "ln1_g": 1.0 + nrm(ks[15], (L, D_MODEL), 0.02),
        "ln1_b": nrm(ks[16], (L, D_MODEL), 0.02),
        "w_route_group": nrm(ks[17], (L, D_MODEL, G), D_MODEL ** -0.5),
        "b_route_group": nrm(ks[18], (L, G), 0.01),
        "w_route_expert": nrm(ks[19], (L, D_MODEL, G * E), D_MODEL ** -0.5),
        "b_route_expert": nrm(ks[20], (L, G * E), 0.01),
        "w_exp_gate": nrm(ks[21], (L, G, E, D_MODEL, F), D_MODEL ** -0.5),
        "w_exp_up": nrm(ks[22], (L, G, E, D_MODEL, F), D_MODEL ** -0.5),
        "w_exp_down": nrm(ks[23], (L, G, E, F, D_MODEL), BETA_INIT * F ** -0.5),
        "ln2_g": 1.0 + nrm(ks[24], (L, D_MODEL), 0.02),
        "ln2_b": nrm(ks[25], (L, D_MODEL), 0.02),
    }


def reference(x_prompt, x_sample, cache_k, cache_v, state_conv, c_prompt, c_sample, w_ada, b_ada, w_in,
              conv_w, conv_b, conv_ln_g, conv_ln_b, w_out, ln1_g, ln1_b, w_route_group, b_route_group,
              w_route_expert, b_route_expert, w_exp_gate, w_exp_up, w_exp_down, ln2_g, ln2_b):
    xp, xs = x_prompt, x_sample
    kp_l, vp_l, cp_l, ks_l, vs_l, cs_l = [], [], [], [], [], []
    for l in range(DEPTH):
        lw = (w_ada[l], b_ada[l], w_in[l], conv_w[l], conv_b[l], conv_ln_g[l], conv_ln_b[l], w_out[l],
              ln1_g[l], ln1_b[l], w_route_group[l], b_route_group[l], w_route_expert[l], b_route_expert[l],
              w_exp_gate[l], w_exp_up[l], w_exp_down[l], ln2_g[l], ln2_b[l])
        bp = xp.shape[0]
        conv_zero = jnp.zeros((bp, CONV_WIDTH - 1, CONV_DIM), xp.dtype)
        kv_empty = jnp.zeros((bp, 0, N_SB_HEADS, HEAD_DIM), xp.dtype)
        xp, cp, kp, vp = trunk_layer(xp, c_prompt, conv_zero, kv_empty, kv_empty, *lw)
        xs, cs, ksn, vsn = trunk_layer(xs, c_sample, state_conv[l], cache_k[l], cache_v[l], *lw)
        kp_l.append(kp); vp_l.append(vp); cp_l.append(cp)
        ks_l.append(ksn); vs_l.append(vsn); cs_l.append(cs)
    return (xp, xs, jnp.stack(kp_l), jnp.stack(vp_l), jnp.stack(cp_l), jnp.stack(ks_l), jnp.stack(vs_l), jnp.stack(cs_l))
```

```python
import functools
import math

import jax
import jax.numpy as jnp
from jax import lax
from jax.experimental import pallas as pl
from jax.experimental.pallas import tpu as pltpu

F32 = jnp.float32
BF16 = jnp.bfloat16

LN_EPS = 1e-5
TOP_K_IN_GROUP = 2
LANES = 128
VMEM_LIMIT_BYTES = 56 * 1024 * 1024
ROW_TILE = 256
ATT_BLOCK = 256
CONV_HALO = 32
CONV_ROWS = 32


def _params(*sem):
    return pltpu.CompilerParams(dimension_semantics=sem, vmem_limit_bytes=VMEM_LIMIT_BYTES)


def _layer_norm(x):
    mu = jnp.mean(x, axis=-1, keepdims=True)
    xc = x - mu
    var = jnp.mean(xc * xc, axis=-1, keepdims=True)
    return xc * lax.rsqrt(var + LN_EPS)


def _sigmoid(x):
    return 1.0 / (1.0 + jnp.exp(-x))


def _split_bf16(x):
    hi = x.astype(BF16)
    lo = (x - hi.astype(F32)).astype(BF16)
    return hi, lo


def _row_blocks(b, t):
    if t >= ROW_TILE:
        assert t % ROW_TILE == 0
        return 1, ROW_TILE
    bb = min(b, ROW_TILE // t)
    assert b % bb == 0
    return bb, t


def _ada_kernel(c_ref, w_ref, b_ref, o_ref):
    c = c_ref[...]
    s = c * _sigmoid(c)
    o_ref[...] = jnp.dot(s.astype(BF16), w_ref[...].astype(BF16), preferred_element_type=F32) + b_ref[...]


def _ada(c, w, b):
    r, d = c.shape
    n = w.shape[1]
    tn = 512
    assert n % tn == 0
    return pl.pallas_call(
        _ada_kernel,
        grid=(n // tn,),
        in_specs=[pl.BlockSpec((r, d), lambda j: (0, 0)),
                  pl.BlockSpec((d, tn), lambda j: (0, j)),
                  pl.BlockSpec((1, tn), lambda j: (0, j))],
        out_specs=pl.BlockSpec((r, tn), lambda j: (0, j)),
        out_shape=jax.ShapeDtypeStruct((r, n), F32),
        compiler_params=_params("arbitrary"),
        name="ada",
    )(c, w, b.reshape(1, n))


def _ln_mod_kernel(x_ref, sh_ref, sc_ref, o_ref):
    y = _layer_norm(x_ref[...])
    o_ref[...] = (y * (1.0 + sc_ref[...]) + sh_ref[...]).astype(o_ref.dtype)


def _ln_mod(x, shift, scale):
    b, t, d = x.shape
    bb, tt = _row_blocks(b, t)
    row = pl.BlockSpec((bb, tt, d), lambda i, j: (i, j, 0))
    per_b = pl.BlockSpec((bb, 1, d), lambda i, j: (i, 0, 0))
    return pl.pallas_call(
        _ln_mod_kernel,
        grid=(b // bb, t // tt),
        in_specs=[row, per_b, per_b],
        out_specs=row,
        out_shape=jax.ShapeDtypeStruct((b, t, d), BF16),
        compiler_params=_params("arbitrary", "arbitrary"),
        name="ln_mod",
    )(x, shift, scale)


def _proj_glu_kernel(h_ref, wv_ref, wg_ref, o_ref):
    h = h_ref[...]
    cv = jnp.dot(h, wv_ref[...], preferred_element_type=F32)
    cg = jnp.dot(h, wg_ref[...], preferred_element_type=F32)
    o_ref[...] = cv * _sigmoid(cg)


def _proj_kernel(h_ref, w_ref, *o_refs):
    y = jnp.dot(h_ref[...], w_ref[...], preferred_element_type=F32)
    for o_ref in o_refs:
        o_ref[...] = y.astype(o_ref.dtype)


def _proj(h, ws, out_dtypes, glu=False):
    m, d = h.shape
    n = ws[0].shape[1]
    tm = min(m, 512)
    assert m % tm == 0
    w_spec = pl.BlockSpec((d, n), lambda i: (0, 0), pipeline_mode=pl.Buffered(1))
    o_spec = pl.BlockSpec((tm, n), lambda i: (i, 0))
    outs = pl.pallas_call(
        _proj_glu_kernel if glu else _proj_kernel,
        grid=(m // tm,),
        in_specs=[pl.BlockSpec((tm, d), lambda i: (i, 0))] + [w_spec] * len(ws),
        out_specs=[o_spec] * len(out_dtypes),
        out_shape=[jax.ShapeDtypeStruct((m, n), dt) for dt in out_dtypes],
        compiler_params=_params("arbitrary"),
        name="proj_glu" if glu else "proj",
    )(h, *ws)
    return outs


def _conv_kernel(a_ref, halo_ref, hist_ref, cw_ref, cb_ref, g_ref, b_ref, o_ref, s_ref, *, width):
    i = pl.program_id(1)
    tt = a_ref.shape[1]

    @pl.when(i == 0)
    def _():
        s_ref[0:CONV_HALO, :] = hist_ref[0]

    @pl.when(i > 0)
    def _():
        s_ref[0:CONV_HALO, :] = halo_ref[0]

    s_ref[CONV_HALO:CONV_HALO + tt, :] = a_ref[0]
    first = CONV_HALO - (width - 1)
    cw = cw_ref[...]
    for r in range(tt // CONV_ROWS):
        base = r * CONV_ROWS
        acc = jnp.broadcast_to(cb_ref[...], (CONV_ROWS, cw.shape[1]))
        for w in range(width):
            acc = acc + s_ref[base + first + w:base + first + w + CONV_ROWS, :] * cw[w:w + 1, :]
        y = _layer_norm(acc) * g_ref[...] + b_ref[...]
        o_ref[0, base:base + CONV_ROWS, :] = (y * _sigmoid(y)).astype(o_ref.dtype)


def _conv(a, hist, cw, cb, g, bt):
    b, t, c = a.shape
    width = cw.shape[0]
    assert width - 1 <= CONV_HALO
    tt = min(t, ROW_TILE)
    assert t % tt == 0 and tt % CONV_HALO == 0
    per_halo = tt // CONV_HALO
    vec = pl.BlockSpec((1, c), lambda i, j: (0, 0))
    return pl.pallas_call(
        functools.partial(_conv_kernel, width=width),
        grid=(b, t // tt),
        in_specs=[pl.BlockSpec((1, tt, c), lambda i, j: (i, j, 0)),
                  pl.BlockSpec((1, CONV_HALO, c), lambda i, j: (i, jnp.maximum(j * per_halo - 1, 0), 0)),
                  pl.BlockSpec((1, CONV_HALO, c), lambda i, j: (i, 0, 0)),
                  pl.BlockSpec((width, c), lambda i, j: (0, 0)),
                  vec, vec, vec],
        out_specs=pl.BlockSpec((1, tt, c), lambda i, j: (i, j, 0)),
        out_shape=jax.ShapeDtypeStruct((b, t, c), BF16),
        scratch_shapes=[pltpu.VMEM((CONV_HALO + tt, c), F32)],
        compiler_params=_params("arbitrary", "arbitrary"),
        name="conv",
    )(a, a, hist, cw, cb.reshape(1, c), g.reshape(1, c), bt.reshape(1, c))


def _sb_block(q, k, v, u, scale, carry, mask):
    z = lax.dot_general(q, k, (((1,), (1,)), ((), ())), preferred_element_type=F32) * scale
    log_keep = -(jnp.maximum(z, 0.0) + jnp.log(1.0 + jnp.exp(-jnp.abs(z))))
    if mask is not None:
        log_keep = jnp.where(mask, log_keep, 0.0)
    hi, lo = _split_bf16(log_keep)
    rc = jnp.dot(hi, u, preferred_element_type=F32) + jnp.dot(lo, u, preferred_element_type=F32)
    a = jnp.exp(z + rc + carry)
    if mask is not None:
        a = jnp.where(mask, a, 0.0)
    out = jnp.dot(a.astype(BF16), v, preferred_element_type=F32)
    return out, carry + rc[:, 0:1]


def _strict_causal_mask(n):
    row = lax.broadcasted_iota(jnp.int32, (n, n), 0)
    col = lax.broadcasted_iota(jnp.int32, (n, n), 1)
    return col < row


def _attn_self_kernel(q_ref, k_ref, v_ref, u_ref, o_ref, acc_ref, carry_ref, *, scale):
    i = pl.program_id(2)
    blk = q_ref.shape[0]
    q = q_ref[...]
    u = u_ref[...]
    start = pl.multiple_of(i * blk, blk)
    out, carry = _sb_block(q, k_ref[pl.ds(start, blk), :], v_ref[pl.ds(start, blk), :], u, scale,
                           jnp.zeros((blk, 1), F32), _strict_causal_mask(blk))
    acc_ref[...] = out
    carry_ref[...] = carry

    def body(j, _):
        s = pl.multiple_of((i - 1 - j) * blk, blk)
        out, carry = _sb_block(q, k_ref[pl.ds(s, blk), :], v_ref[pl.ds(s, blk), :], u, scale,
                               carry_ref[...], None)
        acc_ref[...] += out
        carry_ref[...] = carry
        return 0

    lax.fori_loop(0, i, body, 0)
    o_ref[...] = acc_ref[...].astype(o_ref.dtype)


def _attn_self(q, k, v, u, dh):
    b, t, hd = q.shape
    blk = ATT_BLOCK
    assert t % blk == 0 and hd % dh == 0 and dh % LANES == 0
    full = pl.BlockSpec((None, t, dh), lambda bi, h, i: (bi, 0, h))
    tile = pl.BlockSpec((None, blk, dh), lambda bi, h, i: (bi, i, h))
    return pl.pallas_call(
        functools.partial(_attn_self_kernel, scale=dh ** -0.5),
        grid=(b, hd // dh, t // blk),
        in_specs=[tile, full, full, pl.BlockSpec((blk, blk), lambda bi, h, i: (0, 0))],
        out_specs=tile,
        out_shape=jax.ShapeDtypeStruct((b, t, hd), BF16),
        scratch_shapes=[pltpu.VMEM((blk, dh), F32), pltpu.VMEM((blk, 1), F32)],
        compiler_params=_params("arbitrary", "arbitrary", "arbitrary"),
        name="attn_self",
    )(q, k, v, u)


def _attn_cached_kernel(q_ref, kn_ref, vn_ref, kc_ref, vc_ref, u_ref, o_ref, acc_ref, carry_ref, *, scale, dh):
    s = pl.program_id(1)
    t = q_ref.shape[0]
    n_heads = q_ref.shape[1] // dh
    span = kc_ref.shape[0]
    blk = u_ref.shape[0]

    @pl.when(s == 0)
    def _():
        mask = _strict_causal_mask(t)
        for h in range(n_heads):
            cols = slice(h * dh, (h + 1) * dh)
            out, carry = _sb_block(q_ref[:, cols], kn_ref[:, cols], vn_ref[:, cols], u_ref[0:t, 0:t], scale,
                                   jnp.zeros((t, 1), F32), mask)
            acc_ref[:, cols] = out
            carry_ref[h] = carry

    for h in range(n_heads):
        cols = slice(h * dh, (h + 1) * dh)
        q = q_ref[:, cols]
        carry = carry_ref[h]
        out = jnp.zeros((t, dh), F32)
        for c in reversed(range(span // blk)):
            rows = slice(c * blk, (c + 1) * blk)
            o, carry = _sb_block(q, kc_ref[rows, cols].astype(BF16), vc_ref[rows, cols].astype(BF16), u_ref[...],
                                 scale, carry, None)
            out = out + o
        acc_ref[:, cols] += out
        carry_ref[h] = carry

    @pl.when(s == pl.num_programs(1) - 1)
    def _():
        o_ref[...] = acc_ref[...].astype(o_ref.dtype)


def _attn_cached(q, kn, vn, kc, vc, u, dh):
    b, t, hd = q.shape
    p = kc.shape[1]
    blk = u.shape[0]
    span = min(p, 2 * blk)
    assert p % span == 0 and span % blk == 0 and t <= blk and t % 16 == 0
    n_steps = p // span
    new = pl.BlockSpec((None, t, hd), lambda bi, s: (bi, 0, 0))
    old = pl.BlockSpec((None, span, hd), lambda bi, s: (bi, n_steps - 1 - s, 0))
    return pl.pallas_call(
        functools.partial(_attn_cached_kernel, scale=dh ** -0.5, dh=dh),
        grid=(b, n_steps),
        in_specs=[new, new, new, old, old, pl.BlockSpec((blk, blk), lambda bi, s: (0, 0))],
        out_specs=new,
        out_shape=jax.ShapeDtypeStruct((b, t, hd), BF16),
        scratch_shapes=[pltpu.VMEM((t, hd), F32), pltpu.VMEM((hd // dh, t, 1), F32)],
        compiler_params=_params("arbitrary", "arbitrary"),
        name="attn_cached",
    )(q, kn, vn, kc, vc, u)


def _first_max(cols):
    m = cols[0]
    for c in cols[1:]:
        m = jnp.maximum(m, c)
    hot, found = [], None
    for c in cols:
        is_max = c == m
        if found is None:
            hot.append(is_max)
            found = is_max
        else:
            hot.append(jnp.logical_and(is_max, jnp.logical_not(found)))
            found = jnp.logical_or(found, is_max)
    return m, hot


def _route(logits, n_groups, n_exp):
    lg = [logits[:, g:g + 1] for g in range(n_groups)]
    m, g_hot = _first_max(lg)
    denom = jnp.exp(lg[0] - m)
    for c in lg[1:]:
        denom = denom + jnp.exp(c - m)
    p_top = 1.0 / denom
    le = []
    for e in range(n_exp):
        sel = jnp.zeros_like(m)
        for g in range(n_groups):
            col = n_groups + g * n_exp + e
            sel = jnp.where(g_hot[g], logits[:, col:col + 1], sel)
        le.append(sel)
    v1, hot1 = _first_max(le)
    rest = [jnp.where(h, -jnp.inf, c) for h, c in zip(hot1, le)]
    v2, hot2 = _first_max(rest)
    assert TOP_K_IN_GROUP == 2
    e2 = jnp.exp(v2 - v1)
    w1 = p_top / (1.0 + e2)
    w2 = p_top * e2 / (1.0 + e2)
    lane = lax.broadcasted_iota(jnp.int32, logits.shape, 1)
    gates = jnp.zeros(logits.shape, F32)
    for g in range(n_groups):
        for e in range(n_exp):
            val = jnp.where(hot1[e], w1, jnp.where(hot2[e], w2, 0.0))
            val = jnp.where(g_hot[g], val, 0.0)
            gates = jnp.where(lane == g * n_exp + e, val, gates)
    return gates


def _out_kernel(co_ref, at_ref, wa_ref, wb_ref, x_ref, g1_ref, sc2_ref, sh2_ref, lg_ref, lb_ref, wr_ref, br_ref,
                x1_ref, h2_ref, gt_ref, *, alpha, n_groups, n_exp):
    bb, tt, d = x_ref.shape
    rows = bb * tt
    co = co_ref[...].reshape(rows, co_ref.shape[2])
    at = at_ref[...].reshape(rows, at_ref.shape[2])
    mix = jnp.dot(co, wa_ref[...], preferred_element_type=F32) + jnp.dot(at, wb_ref[...], preferred_element_type=F32)
    mix = mix.reshape(bb, tt, d)
    x1 = _layer_norm(alpha * x_ref[...] + (1.0 + g1_ref[...]) * mix) * lg_ref[...] + lb_ref[...]
    x1_ref[...] = x1
    h2 = _layer_norm(x1) * (1.0 + sc2_ref[...]) + sh2_ref[...]
    h2_ref[...] = h2.astype(h2_ref.dtype)
    h_hi, h_lo = _split_bf16(h2.reshape(rows, d))
    w_hi, w_lo = _split_bf16(wr_ref[...])
    logits = (jnp.dot(h_hi, w_hi, preferred_element_type=F32) + jnp.dot(h_lo, w_hi, preferred_element_type=F32)
              + jnp.dot(h_hi, w_lo, preferred_element_type=F32)) + br_ref[...]
    gt_ref[...] = _route(logits, n_groups, n_exp).reshape(bb, tt, LANES)


def _out_proj(co, at, wa, wb, x, gate1, scale2, shift2, ln_g, ln_b, w_route, b_route, alpha, n_groups, n_exp):
    b, t, d = x.shape
    c = co.shape[2]
    bb, tt = _row_blocks(b, t)
    row = lambda n: pl.BlockSpec((bb, tt, n), lambda i, j: (i, j, 0))
    per_b = pl.BlockSpec((bb, 1, d), lambda i, j: (i, 0, 0))
    const = lambda r, n: pl.BlockSpec((r, n), lambda i, j: (0, 0), pipeline_mode=pl.Buffered(1))
    return pl.pallas_call(
        functools.partial(_out_kernel, alpha=alpha, n_groups=n_groups, n_exp=n_exp),
        grid=(b // bb, t // tt),
        in_specs=[row(c), row(at.shape[2]), const(c, d), const(at.shape[2], d), row(d), per_b, per_b, per_b,
                  const(1, d), const(1, d), const(d, LANES), const(1, LANES)],
        out_specs=[row(d), row(d), row(LANES)],
        out_shape=[jax.ShapeDtypeStruct((b, t, d), F32), jax.ShapeDtypeStruct((b, t, d), BF16),
                   jax.ShapeDtypeStruct((b, t, LANES), F32)],
        compiler_params=_params("arbitrary", "arbitrary"),
        name="out_proj",
    )(co, at, wa, wb, x, gate1, scale2, shift2, ln_g.reshape(1, d), ln_b.reshape(1, d), w_route, b_route)


def _moe_kernel(h_ref, gt_ref, wg_ref, wu_ref, wd_ref, x1_ref, g2_ref, lg_ref, lb_ref, y_ref, *, alpha, n_groups,
                n_exp):
    g = pl.program_id(2)
    bb, tt, d = h_ref.shape
    rows = bb * tt
    h = h_ref[...].reshape(rows, d)
    a = jnp.dot(h, wg_ref[...], preferred_element_type=F32)
    up = jnp.dot(h, wu_ref[...], preferred_element_type=F32)
    f = a.shape[1] // n_exp
    gates = gt_ref[...].reshape(rows, LANES)
    parts = []
    for e in range(n_exp):
        col = jnp.zeros((rows, 1), F32)
        for gp in range(n_groups):
            col = jnp.where(g == gp, gates[:, gp * n_exp + e:gp * n_exp + e + 1], col)
        ae = a[:, e * f:(e + 1) * f]
        parts.append(ae * _sigmoid(ae) * up[:, e * f:(e + 1) * f] * col)
    act = jnp.concatenate(parts, axis=1).astype(BF16)
    ff = jnp.dot(act, wd_ref[...], preferred_element_type=F32).reshape(bb, tt, d)

    @pl.when(g == 0)
    def _():
        y_ref[...] = ff

    @pl.when(jnp.logical_and(g > 0, g < n_groups - 1))
    def _():
        y_ref[...] += ff

    @pl.when(g == n_groups - 1)
    def _():
        total = y_ref[...] + ff
        y_ref[...] = _layer_norm(alpha * x1_ref[...] + (1.0 + g2_ref[...]) * total) * lg_ref[...] + lb_ref[...]


def _moe(h2, gates, wg, wu, wd, x1, gate2, ln_g, ln_b, alpha, n_exp):
    b, t, d = x1.shape
    n_groups, _, ef = wg.shape
    assert n_groups > 1 and n_groups * n_exp <= LANES
    bb, tt = _row_blocks(b, t)
    row = lambda n: pl.BlockSpec((bb, tt, n), lambda i, j, g: (i, j, 0))
    per_b = pl.BlockSpec((bb, 1, d), lambda i, j, g: (i, 0, 0))
    vec = pl.BlockSpec((1, d), lambda i, j, g: (0, 0))
    return pl.pallas_call(
        functools.partial(_moe_kernel, alpha=alpha, n_groups=n_groups, n_exp=n_exp),
        grid=(b // bb, t // tt, n_groups),
        in_specs=[row(d), row(LANES),
                  pl.BlockSpec((None, d, ef), lambda i, j, g: (g, 0, 0)),
                  pl.BlockSpec((None, d, ef), lambda i, j, g: (g, 0, 0)),
                  pl.BlockSpec((None, ef, d), lambda i, j, g: (g, 0, 0)),
                  row(d), per_b, vec, vec],
        out_specs=row(d),
        out_shape=jax.ShapeDtypeStruct((b, t, d), F32),
        compiler_params=_params("arbitrary", "arbitrary", "arbitrary"),
        name="moe",
    )(h2, gates, wg, wu, wd, x1, gate2, ln_g.reshape(1, d), ln_b.reshape(1, d))


def _trunk_layer(x, mod, hist, cache, w, dh, alpha):
    b, t, d = x.shape
    m = b * t
    mods = [mod[:, i:i + 1] for i in range(6)]
    shift1, scale1, gate1, shift2, scale2, gate2 = mods
    h1 = _ln_mod(x, shift1, scale1).reshape(m, d)
    (a_glu,) = _proj(h1, (w["cv"], w["cg"]), (F32,), glu=True)
    (q,) = _proj(h1, (w["q"],), (BF16,))
    k, k_bf = _proj(h1, (w["k"],), (F32, BF16))
    v, v_bf = _proj(h1, (w["v"],), (F32, BF16))
    c = a_glu.shape[1]
    hd = q.shape[1]
    a_glu = a_glu.reshape(b, t, c)
    conv_out = _conv(a_glu, hist, w["conv_w"], w["conv_b"], w["conv_ln_g"], w["conv_ln_b"])
    q3, k3, v3 = (z.reshape(b, t, hd) for z in (q, k_bf, v_bf))
    if cache is None:
        attn = _attn_self(q3, k3, v3, w["tri"], dh)
    else:
        attn = _attn_cached(q3, k3, v3, cache[0], cache[1], w["tri"], dh)
    x1, h2, gates = _out_proj(conv_out, attn, w["out_a"], w["out_b"], x, gate1, scale2, shift2, w["ln1_g"],
                              w["ln1_b"], w["route_w"], w["route_b"], alpha, w["n_groups"], w["n_exp"])
    y = _moe(h2, gates, w["exp_gate"], w["exp_up"], w["exp_down"], x1, gate2, w["ln2_g"], w["ln2_b"], alpha,
             w["n_exp"])
    n_hist = w["conv_w"].shape[0] - 1
    conv_state = jnp.concatenate([hist[:, CONV_HALO - n_hist:], a_glu], axis=1)[:, -n_hist:] if t < n_hist \
        else a_glu[:, t - n_hist:]
    n_heads = hd // dh
    return y, conv_state, k.reshape(b, t, n_heads, dh), v.reshape(b, t, n_heads, dh)


def kernel(x_prompt, x_sample, cache_k, cache_v, state_conv, c_prompt, c_sample, w_ada, b_ada, w_in, conv_w, conv_b, conv_ln_g, conv_ln_b, w_out, ln1_g, ln1_b, w_route_group, b_route_group, w_route_expert, b_route_expert, w_exp_gate, w_exp_up, w_exp_down, ln2_g, ln2_b):
    depth = w_ada.shape[0]
    d = x_prompt.shape[2]
    bp = x_prompt.shape[0]
    bs = x_sample.shape[0]
    dh = cache_k.shape[4]
    hd = cache_k.shape[3] * dh
    c = conv_w.shape[2]
    n_groups, n_exp, _, f = w_exp_gate.shape[1:]
    alpha = (2.0 * depth) ** 0.25
    n_hist = conv_w.shape[1] - 1
    tri = jnp.tril(jnp.ones((ATT_BLOCK, ATT_BLOCK), BF16))

    xp, xs = x_prompt, x_sample
    outs = [[] for _ in range(6)]
    for l in range(depth):
        w_in_bf = w_in[l].astype(BF16)
        n_route = n_groups * (1 + n_exp)
        route_w = jnp.concatenate([w_route_group[l], w_route_expert[l], jnp.zeros((d, LANES - n_route), F32)], axis=1)
        route_b = jnp.concatenate([b_route_group[l], b_route_expert[l], jnp.zeros((LANES - n_route,), F32)])
        w = {
            "cv": w_in_bf[:, :c], "cg": w_in_bf[:, c:2 * c], "q": w_in_bf[:, 2 * c:2 * c + hd],
            "k": w_in_bf[:, 2 * c + hd:2 * c + 2 * hd], "v": w_in_bf[:, 2 * c + 2 * hd:],
            "conv_w": conv_w[l], "conv_b": conv_b[l], "conv_ln_g": conv_ln_g[l], "conv_ln_b": conv_ln_b[l],
            "out_a": w_out[l, :c].astype(BF16), "out_b": w_out[l, c:].astype(BF16),
            "ln1_g": ln1_g[l], "ln1_b": ln1_b[l], "ln2_g": ln2_g[l], "ln2_b": ln2_b[l],
            "route_w": route_w, "route_b": route_b.reshape(1, LANES),
            "exp_gate": w_exp_gate[l].astype(BF16).transpose(0, 2, 1, 3).reshape(n_groups, d, n_exp * f),
            "exp_up": w_exp_up[l].astype(BF16).transpose(0, 2, 1, 3).reshape(n_groups, d, n_exp * f),
            "exp_down": w_exp_down[l].astype(BF16).reshape(n_groups, n_exp * f, d),
            "tri": tri, "n_groups": n_groups, "n_exp": n_exp,
        }
        rows = bp + bs
        pad = (-rows) % 16
        cond = jnp.concatenate([c_prompt, c_sample, jnp.zeros((pad, d), F32)], axis=0)
        mod = _ada(cond, w_ada[l], b_ada[l])[:rows].reshape(rows, 6, d)
        t_s = xs.shape[1]
        hist_p = jnp.zeros((bp, CONV_HALO, c), F32)
        hist_s = jnp.concatenate([jnp.zeros((bs, CONV_HALO - n_hist, c), F32), state_conv[l]], axis=1)
        cache = (cache_k[l].reshape(bs, -1, hd), cache_v[l].reshape(bs, -1, hd))
        xp, cp, kp, vp = _trunk_layer(xp, mod[:bp], hist_p, None, w, dh, alpha)
        xs, cs, ks, vs = _trunk_layer(xs, mod[bp:], hist_s, cache, w, dh, alpha)
        for lst, val in zip(outs, (kp, vp, cp, ks, vs, cs)):
            lst.append(val)
    return (xp, xs) + tuple(jnp.stack(o) for o in outs)
```

```python
import functools
import math

import jax
import jax.numpy as jnp
from jax import lax
from jax.experimental import pallas as pl
from jax.experimental.pallas import tpu as pltpu

F32 = jnp.float32
BF16 = jnp.bfloat16

LN_EPS = 1e-5
TOP_K_IN_GROUP = 2
LANES = 128
VMEM_LIMIT_BYTES = 56 * 1024 * 1024
ROW_TILE = 256
ATT_BLOCK = 256
ATT_HEADS_PER_STEP = 4
LOG2E = math.log2(math.e)
CONV_HALO = 32
CONV_ROWS = 32


def _params(*sem):
    return pltpu.CompilerParams(dimension_semantics=sem, vmem_limit_bytes=VMEM_LIMIT_BYTES)


def _layer_norm(x):
    mu = jnp.mean(x, axis=-1, keepdims=True)
    xc = x - mu
    var = jnp.mean(xc * xc, axis=-1, keepdims=True)
    return xc * lax.rsqrt(var + LN_EPS)


def _sigmoid(x):
    return 1.0 / (1.0 + jnp.exp(-x))


def _split_bf16(x):
    hi = x.astype(BF16)
    lo = (x - hi.astype(F32)).astype(BF16)
    return hi, lo


def _row_blocks(b, t):
    if t >= ROW_TILE:
        assert t % ROW_TILE == 0
        return 1, ROW_TILE
    bb = min(b, ROW_TILE // t)
    assert b % bb == 0
    return bb, t


def _ada_kernel(c_ref, w_ref, b_ref, o_ref):
    c = c_ref[...]
    s = c * _sigmoid(c)
    o_ref[...] = jnp.dot(s.astype(BF16), w_ref[...].astype(BF16), preferred_element_type=F32) + b_ref[...]


def _ada(c, w, b):
    r, d = c.shape
    n = w.shape[1]
    tn = 512
    assert n % tn == 0
    return pl.pallas_call(
        _ada_kernel,
        grid=(n // tn,),
        in_specs=[pl.BlockSpec((r, d), lambda j: (0, 0)),
                  pl.BlockSpec((d, tn), lambda j: (0, j)),
                  pl.BlockSpec((1, tn), lambda j: (0, j))],
        out_specs=pl.BlockSpec((r, tn), lambda j: (0, j)),
        out_shape=jax.ShapeDtypeStruct((r, n), F32),
        compiler_params=_params("arbitrary"),
        name="ada",
    )(c, w, b.reshape(1, n))


def _ln_mod_kernel(x_ref, sh_ref, sc_ref, o_ref):
    y = _layer_norm(x_ref[...])
    o_ref[...] = (y * (1.0 + sc_ref[...]) + sh_ref[...]).astype(o_ref.dtype)


def _ln_mod(x, shift, scale):
    b, t, d = x.shape
    bb, tt = _row_blocks(b, t)
    row = pl.BlockSpec((bb, tt, d), lambda i, j: (i, j, 0))
    per_b = pl.BlockSpec((bb, 1, d), lambda i, j: (i, 0, 0))
    return pl.pallas_call(
        _ln_mod_kernel,
        grid=(b // bb, t // tt),
        in_specs=[row, per_b, per_b],
        out_specs=row,
        out_shape=jax.ShapeDtypeStruct((b, t, d), BF16),
        compiler_params=_params("arbitrary", "arbitrary"),
        name="ln_mod",
    )(x, shift, scale)


def _proj_glu_kernel(h_ref, wv_ref, wg_ref, o_ref):
    h = h_ref[...]
    cv = jnp.dot(h, wv_ref[...], preferred_element_type=F32)
    cg = jnp.dot(h, wg_ref[...], preferred_element_type=F32)
    o_ref[...] = cv * _sigmoid(cg)


def _proj_kernel(h_ref, w_ref, *o_refs, scale):
    y = jnp.dot(h_ref[...], w_ref[...], preferred_element_type=F32)
    if scale != 1.0:
        y = y * scale
    for o_ref in o_refs:
        o_ref[...] = y.astype(o_ref.dtype)


def _proj(h, ws, out_dtypes, glu=False, scale=1.0):
    m, d = h.shape
    n = ws[0].shape[1]
    tm = min(m, 512)
    assert m % tm == 0
    w_spec = pl.BlockSpec((d, n), lambda i: (0, 0), pipeline_mode=pl.Buffered(1))
    o_spec = pl.BlockSpec((tm, n), lambda i: (i, 0))
    outs = pl.pallas_call(
        _proj_glu_kernel if glu else functools.partial(_proj_kernel, scale=scale),
        grid=(m // tm,),
        in_specs=[pl.BlockSpec((tm, d), lambda i: (i, 0))] + [w_spec] * len(ws),
        out_specs=[o_spec] * len(out_dtypes),
        out_shape=[jax.ShapeDtypeStruct((m, n), dt) for dt in out_dtypes],
        compiler_params=_params("arbitrary"),
        name="proj_glu" if glu else "proj",
    )(h, *ws)
    return outs


def _conv_kernel(a_ref, halo_ref, hist_ref, cw_ref, cb_ref, g_ref, b_ref, o_ref, s_ref, *, width):
    i = pl.program_id(1)
    tt = a_ref.shape[1]

    @pl.when(i == 0)
    def _():
        s_ref[0:CONV_HALO, :] = hist_ref[0]

    @pl.when(i > 0)
    def _():
        s_ref[0:CONV_HALO, :] = halo_ref[0]

    s_ref[CONV_HALO:CONV_HALO + tt, :] = a_ref[0]
    first = CONV_HALO - (width - 1)
    cw = cw_ref[...]
    for r in range(tt // CONV_ROWS):
        base = r * CONV_ROWS
        acc = jnp.broadcast_to(cb_ref[...], (CONV_ROWS, cw.shape[1]))
        for w in range(width):
            acc = acc + s_ref[base + first + w:base + first + w + CONV_ROWS, :] * cw[w:w + 1, :]
        y = _layer_norm(acc) * g_ref[...] + b_ref[...]
        o_ref[0, base:base + CONV_ROWS, :] = (y * _sigmoid(y)).astype(o_ref.dtype)


def _conv(a, hist, cw, cb, g, bt):
    b, t, c = a.shape
    width = cw.shape[0]
    assert width - 1 <= CONV_HALO
    tt = min(t, ROW_TILE)
    assert t % tt == 0 and tt % CONV_HALO == 0
    per_halo = tt // CONV_HALO
    vec = pl.BlockSpec((1, c), lambda i, j: (0, 0))
    return pl.pallas_call(
        functools.partial(_conv_kernel, width=width),
        grid=(b, t // tt),
        in_specs=[pl.BlockSpec((1, tt, c), lambda i, j: (i, j, 0)),
                  pl.BlockSpec((1, CONV_HALO, c), lambda i, j: (i, jnp.maximum(j * per_halo - 1, 0), 0)),
                  pl.BlockSpec((1, CONV_HALO, c), lambda i, j: (i, 0, 0)),
                  pl.BlockSpec((width, c), lambda i, j: (0, 0)),
                  vec, vec, vec],
        out_specs=pl.BlockSpec((1, tt, c), lambda i, j: (i, j, 0)),
        out_shape=jax.ShapeDtypeStruct((b, t, c), BF16),
        scratch_shapes=[pltpu.VMEM((CONV_HALO + tt, c), F32)],
        compiler_params=_params("arbitrary", "arbitrary"),
        name="conv",
    )(a, a, hist, cw, cb.reshape(1, c), g.reshape(1, c), bt.reshape(1, c))


def _softplus2(z):
    neg_abs = lax.bitcast_convert_type(lax.bitcast_convert_type(z, jnp.uint32) | jnp.uint32(0x80000000), F32)
    return jnp.maximum(z, 0.0) + jnp.log(1.0 + jnp.exp2(neg_abs)) * LOG2E


def _sb_logits(qs, ks):
    nt = (((1,), (1,)), ((), ()))
    return jnp.concatenate([lax.dot_general(q, k, nt, preferred_element_type=F32) for q, k in zip(qs, ks)], axis=0)


def _sb_finish(z, vs, u_neg, carry, causal):
    tq, tk = z.shape[0] // len(vs), z.shape[1]
    sp = _softplus2(z)
    if causal:
        assert tq & (tq - 1) == 0
        keep = lax.broadcasted_iota(jnp.int32, z.shape, 1) < (lax.broadcasted_iota(jnp.int32, z.shape, 0) & (tq - 1))
        sp = jnp.where(keep, sp, 0.0)
    hi, lo = _split_bf16(sp)
    if u_neg.shape[0] == 2 * tk:
        rc = jnp.dot(jnp.concatenate([hi, lo], axis=1), u_neg, preferred_element_type=F32)
    else:
        rc = jnp.dot(hi, u_neg, preferred_element_type=F32) + jnp.dot(lo, u_neg, preferred_element_type=F32)
    a = jnp.exp2(z + rc + carry)
    if causal:
        a = jnp.where(keep, a, 0.0)
    a = a.astype(BF16)
    outs = [jnp.dot(a[c * tq:(c + 1) * tq], v, preferred_element_type=F32) for c, v in enumerate(vs)]
    return outs, carry + rc[:, 0:1]


def _attn_self_kernel(q_ref, k_ref, v_ref, u2_ref, o_ref, acc_ref, carry_ref, z0_ref, z1_ref, *, dh):
    i = pl.program_id(2)
    blk = q_ref.shape[0]
    heads = [slice(h * dh, (h + 1) * dh) for h in range(q_ref.shape[1] // dh)]

    def logits(block):
        rows = pl.ds(pl.multiple_of(jnp.maximum(block, 0) * blk, blk), blk)
        return _sb_logits([q_ref[:, c] for c in heads], [k_ref[rows, c] for c in heads])

    def sweep(z, block):
        rows = pl.ds(pl.multiple_of(block * blk, blk), blk)
        outs, carry = _sb_finish(z, [v_ref[rows, c] for c in heads], u2_ref[...], carry_ref[...], False)
        for c, o in zip(heads, outs):
            acc_ref[:, c] += o
        carry_ref[...] = carry

    z0_ref[...] = logits(i - 1)
    rows = pl.ds(pl.multiple_of(i * blk, blk), blk)
    outs, carry = _sb_finish(logits(i), [v_ref[rows, c] for c in heads], u2_ref[...],
                             jnp.zeros((len(heads) * blk, 1), F32), True)
    for c, o in zip(heads, outs):
        acc_ref[:, c] = o
    carry_ref[...] = carry

    def pair(j, _):
        block = i - 1 - 2 * j
        z1_ref[...] = logits(block - 1)
        sweep(z0_ref[...], block)
        z0_ref[...] = logits(block - 2)
        sweep(z1_ref[...], block - 1)
        return 0

    lax.fori_loop(0, i // 2, pair, 0)

    @pl.when(i % 2 == 1)
    def _():
        sweep(z0_ref[...], 0)

    o_ref[...] = acc_ref[...].astype(o_ref.dtype)


def _attn_self(q, k, v, u2, dh):
    b, t, hd = q.shape
    blk = u2.shape[1]
    n_heads = hd // dh
    per_step = math.gcd(n_heads, ATT_HEADS_PER_STEP)
    assert t % blk == 0 and hd % dh == 0 and dh % LANES == 0
    full = pl.BlockSpec((None, t, per_step * dh), lambda bi, h, i: (bi, 0, h), pipeline_mode=pl.Buffered(1))
    tile = pl.BlockSpec((None, blk, per_step * dh), lambda bi, h, i: (bi, i, h))
    return pl.pallas_call(
        functools.partial(_attn_self_kernel, dh=dh),
        grid=(b, n_heads // per_step, t // blk),
        in_specs=[tile, full, full, pl.BlockSpec(u2.shape, lambda bi, h, i: (0, 0))],
        out_specs=tile,
        out_shape=jax.ShapeDtypeStruct((b, t, hd), BF16),
        scratch_shapes=[pltpu.VMEM((blk, per_step * dh), F32), pltpu.VMEM((per_step * blk, 1), F32),
                        pltpu.VMEM((per_step * blk, blk), F32), pltpu.VMEM((per_step * blk, blk), F32)],
        compiler_params=_params("arbitrary", "arbitrary", "arbitrary"),
        name="attn_self",
    )(q, k, v, u2)


def _attn_cached_kernel(q_ref, kn_ref, vn_ref, kc_ref, vc_ref, u2_ref, o_ref, acc_ref, carry_ref, *, dh):
    s = pl.program_id(1)
    t, hd = q_ref.shape
    heads = [slice(h * dh, (h + 1) * dh) for h in range(hd // dh)]
    span = kc_ref.shape[0]
    blk = u2_ref.shape[1]
    qs = [q_ref[:, c] for c in heads]

    @pl.when(s == 0)
    def _():
        outs, carry = _sb_finish(_sb_logits(qs, [kn_ref[:, c] for c in heads]), [vn_ref[:, c] for c in heads],
                                 u2_ref[0:t, 0:t], jnp.zeros((len(heads) * t, 1), F32), True)
        for c, o in zip(heads, outs):
            acc_ref[:, c] = o
        carry_ref[...] = carry

    carry = carry_ref[...]
    total = None
    for cb in reversed(range(span // blk)):
        rows = slice(cb * blk, (cb + 1) * blk)
        outs, carry = _sb_finish(_sb_logits(qs, [kc_ref[rows, c].astype(BF16) for c in heads]),
                                 [vc_ref[rows, c].astype(BF16) for c in heads], u2_ref[...], carry, False)
        total = outs if total is None else [x + y for x, y in zip(total, outs)]
    for c, o in zip(heads, total):
        acc_ref[:, c] += o
    carry_ref[...] = carry

    @pl.when(s == pl.num_programs(1) - 1)
    def _():
        o_ref[...] = acc_ref[...].astype(o_ref.dtype)


def _attn_cached(q, kn, vn, kc, vc, u2, dh, first):
    b, t, hd = q.shape
    p = kc.shape[1]
    blk = u2.shape[1]
    span = min(p, 2 * blk)
    assert p % span == 0 and span % blk == 0 and t <= blk and t % 16 == 0
    n_steps = p // span
    new = pl.BlockSpec((None, t, hd), lambda bi, s: (bi, 0, 0))
    old = pl.BlockSpec((None, span, hd), lambda bi, s: (first + bi, n_steps - 1 - s, 0))
    return pl.pallas_call(
        functools.partial(_attn_cached_kernel, dh=dh),
        grid=(b, n_steps),
        in_specs=[new, new, new, old, old, pl.BlockSpec(u2.shape, lambda bi, s: (0, 0))],
        out_specs=new,
        out_shape=jax.ShapeDtypeStruct((b, t, hd), BF16),
        scratch_shapes=[pltpu.VMEM((t, hd), F32), pltpu.VMEM((hd // dh * t, 1), F32)],
        compiler_params=_params("arbitrary", "arbitrary"),
        name="attn_cached",
    )(q, kn, vn, kc, vc, u2)


def _first_max(cols):
    m = cols[0]
    for c in cols[1:]:
        m = jnp.maximum(m, c)
    hot, found = [], None
    for c in cols:
        is_max = c == m
        if found is None:
            hot.append(is_max)
            found = is_max
        else:
            hot.append(jnp.logical_and(is_max, jnp.logical_not(found)))
            found = jnp.logical_or(found, is_max)
    return m, hot


def _route(logits, n_groups, n_exp):
    lg = [logits[:, g:g + 1] for g in range(n_groups)]
    m, g_hot = _first_max(lg)
    denom = jnp.exp(lg[0] - m)
    for c in lg[1:]:
        denom = denom + jnp.exp(c - m)
    p_top = 1.0 / denom
    le = []
    for e in range(n_exp):
        sel = jnp.zeros_like(m)
        for g in range(n_groups):
            col = n_groups + g * n_exp + e
            sel = jnp.where(g_hot[g], logits[:, col:col + 1], sel)
        le.append(sel)
    v1, hot1 = _first_max(le)
    rest = [jnp.where(h, -jnp.inf, c) for h, c in zip(hot1, le)]
    v2, hot2 = _first_max(rest)
    assert TOP_K_IN_GROUP == 2
    e2 = jnp.exp(v2 - v1)
    w1 = p_top / (1.0 + e2)
    w2 = p_top * e2 / (1.0 + e2)
    lane = lax.broadcasted_iota(jnp.int32, logits.shape, 1)
    gates = jnp.zeros(logits.shape, F32)
    for g in range(n_groups):
        for e in range(n_exp):
            val = jnp.where(hot1[e], w1, jnp.where(hot2[e], w2, 0.0))
            val = jnp.where(g_hot[g], val, 0.0)
            gates = jnp.where(lane == g * n_exp + e, val, gates)
    return gates


def _out_kernel(co_ref, at_ref, wa_ref, wb_ref, x_ref, g1_ref, sc2_ref, sh2_ref, lg_ref, lb_ref, wr_ref, br_ref,
                x1_ref, h2_ref, gt_ref, *, alpha, n_groups, n_exp):
    bb, tt, d = x_ref.shape
    rows = bb * tt
    co = co_ref[...].reshape(rows, co_ref.shape[2])
    at = at_ref[...].reshape(rows, at_ref.shape[2])
    mix = jnp.dot(co, wa_ref[...], preferred_element_type=F32) + jnp.dot(at, wb_ref[...], preferred_element_type=F32)
    mix = mix.reshape(bb, tt, d)
    x1 = _layer_norm(alpha * x_ref[...] + (1.0 + g1_ref[...]) * mix) * lg_ref[...] + lb_ref[...]
    x1_ref[...] = x1
    h2 = _layer_norm(x1) * (1.0 + sc2_ref[...]) + sh2_ref[...]
    h2_ref[...] = h2.astype(h2_ref.dtype)
    h_hi, h_lo = _split_bf16(h2.reshape(rows, d))
    w_hi, w_lo = _split_bf16(wr_ref[...])
    logits = (jnp.dot(h_hi, w_hi, preferred_element_type=F32) + jnp.dot(h_lo, w_hi, preferred_element_type=F32)
              + jnp.dot(h_hi, w_lo, preferred_element_type=F32)) + br_ref[...]
    gt_ref[...] = _route(logits, n_groups, n_exp).reshape(bb, tt, LANES)


def _out_proj(co, at, wa, wb, x, gate1, scale2, shift2, ln_g, ln_b, w_route, b_route, alpha, n_groups, n_exp):
    b, t, d = x.shape
    c = co.shape[2]
    bb, tt = _row_blocks(b, t)
    row = lambda n: pl.BlockSpec((bb, tt, n), lambda i, j: (i, j, 0))
    per_b = pl.BlockSpec((bb, 1, d), lambda i, j: (i, 0, 0))
    const = lambda r, n: pl.BlockSpec((r, n), lambda i, j: (0, 0), pipeline_mode=pl.Buffered(1))
    return pl.pallas_call(
        functools.partial(_out_kernel, alpha=alpha, n_groups=n_groups, n_exp=n_exp),
        grid=(b // bb, t // tt),
        in_specs=[row(c), row(at.shape[2]), const(c, d), const(at.shape[2], d), row(d), per_b, per_b, per_b,
                  const(1, d), const(1, d), const(d, LANES), const(1, LANES)],
        out_specs=[row(d), row(d), row(LANES)],
        out_shape=[jax.ShapeDtypeStruct((b, t, d), F32), jax.ShapeDtypeStruct((b, t, d), BF16),
                   jax.ShapeDtypeStruct((b, t, LANES), F32)],
        compiler_params=_params("arbitrary", "arbitrary"),
        name="out_proj",
    )(co, at, wa, wb, x, gate1, scale2, shift2, ln_g.reshape(1, d), ln_b.reshape(1, d), w_route, b_route)


def _moe_kernel(h_ref, gt_ref, wg_ref, wu_ref, wd_ref, x1_ref, g2_ref, lg_ref, lb_ref, y_ref, *, alpha, n_groups,
                n_exp):
    g = pl.program_id(2)
    bb, tt, d = h_ref.shape
    rows = bb * tt
    h = h_ref[...].reshape(rows, d)
    a = jnp.dot(h, wg_ref[...], preferred_element_type=F32)
    up = jnp.dot(h, wu_ref[...], preferred_element_type=F32)
    f = a.shape[1] // n_exp
    gates = gt_ref[...].reshape(rows, LANES)
    parts = []
    for e in range(n_exp):
        col = jnp.zeros((rows, 1), F32)
        for gp in range(n_groups):
            col = jnp.where(g == gp, gates[:, gp * n_exp + e:gp * n_exp + e + 1], col)
        ae = a[:, e * f:(e + 1) * f]
        parts.append(ae * _sigmoid(ae) * up[:, e * f:(e + 1) * f] * col)
    act = jnp.concatenate(parts, axis=1).astype(BF16)
    ff = jnp.dot(act, wd_ref[...], preferred_element_type=F32).reshape(bb, tt, d)

    @pl.when(g == 0)
    def _():
        y_ref[...] = ff

    @pl.when(jnp.logical_and(g > 0, g < n_groups - 1))
    def _():
        y_ref[...] += ff

    @pl.when(g == n_groups - 1)
    def _():
        total = y_ref[...] + ff
        y_ref[...] = _layer_norm(alpha * x1_ref[...] + (1.0 + g2_ref[...]) * total) * lg_ref[...] + lb_ref[...]


def _moe(h2, gates, wg, wu, wd, x1, gate2, ln_g, ln_b, alpha, n_exp):
    b, t, d = x1.shape
    n_groups, _, ef = wg.shape
    assert n_groups > 1 and n_groups * n_exp <= LANES
    bb, tt = _row_blocks(b, t)
    row = lambda n: pl.BlockSpec((bb, tt, n), lambda i, j, g: (i, j, 0))
    per_b = pl.BlockSpec((bb, 1, d), lambda i, j, g: (i, 0, 0))
    vec = pl.BlockSpec((1, d), lambda i, j, g: (0, 0))
    return pl.pallas_call(
        functools.partial(_moe_kernel, alpha=alpha, n_groups=n_groups, n_exp=n_exp),
        grid=(b // bb, t // tt, n_groups),
        in_specs=[row(d), row(LANES),
                  pl.BlockSpec((None, d, ef), lambda i, j, g: (g, 0, 0)),
                  pl.BlockSpec((None, d, ef), lambda i, j, g: (g, 0, 0)),
                  pl.BlockSpec((None, ef, d), lambda i, j, g: (g, 0, 0)),
                  row(d), per_b, vec, vec],
        out_specs=row(d),
        out_shape=jax.ShapeDtypeStruct((b, t, d), F32),
        compiler_params=_params("arbitrary", "arbitrary", "arbitrary"),
        name="moe",
    )(h2, gates, wg, wu, wd, x1, gate2, ln_g.reshape(1, d), ln_b.reshape(1, d))


def _trunk_layer(x, mod, hist, cache, w, dh, alpha):
    b, t, d = x.shape
    m = b * t
    mods = [mod[:, i:i + 1] for i in range(6)]
    shift1, scale1, gate1, shift2, scale2, gate2 = mods
    h1 = _ln_mod(x, shift1, scale1).reshape(m, d)
    (a_glu,) = _proj(h1, (w["cv"], w["cg"]), (F32,), glu=True)
    (q,) = _proj(h1, (w["q"],), (BF16,), scale=dh ** -0.5 * LOG2E)
    k, k_bf = _proj(h1, (w["k"],), (F32, BF16))
    v, v_bf = _proj(h1, (w["v"],), (F32, BF16))
    c = a_glu.shape[1]
    hd = q.shape[1]
    a_glu = a_glu.reshape(b, t, c)
    conv_out = _conv(a_glu, hist, w["conv_w"], w["conv_b"], w["conv_ln_g"], w["conv_ln_b"])
    q3, k3, v3 = (z.reshape(b, t, hd) for z in (q, k_bf, v_bf))
    if cache is None:
        attn = _attn_self(q3, k3, v3, w["tri"], dh)
    else:
        attn = _attn_cached(q3, k3, v3, cache[0], cache[1], w["tri"], dh, cache[2])
    x1, h2, gates = _out_proj(conv_out, attn, w["out_a"], w["out_b"], x, gate1, scale2, shift2, w["ln1_g"],
                              w["ln1_b"], w["route_w"], w["route_b"], alpha, w["n_groups"], w["n_exp"])
    y = _moe(h2, gates, w["exp_gate"], w["exp_up"], w["exp_down"], x1, gate2, w["ln2_g"], w["ln2_b"], alpha,
             w["n_exp"])
    n_hist = w["conv_w"].shape[0] - 1
    assert t >= n_hist
    conv_state = a_glu[:, t - n_hist:]
    n_heads = hd // dh
    return y, conv_state, k.reshape(b, t, n_heads, dh), v.reshape(b, t, n_heads, dh)


def kernel(x_prompt, x_sample, cache_k, cache_v, state_conv, c_prompt, c_sample, w_ada, b_ada, w_in, conv_w, conv_b, conv_ln_g, conv_ln_b, w_out, ln1_g, ln1_b, w_route_group, b_route_group, w_route_expert, b_route_expert, w_exp_gate, w_exp_up, w_exp_down, ln2_g, ln2_b):
    depth = w_ada.shape[0]
    d = x_prompt.shape[2]
    bp = x_prompt.shape[0]
    bs = x_sample.shape[0]
    dh = cache_k.shape[4]
    hd = cache_k.shape[3] * dh
    c = conv_w.shape[2]
    n_groups, n_exp, _, f = w_exp_gate.shape[1:]
    alpha = (2.0 * depth) ** 0.25
    n_hist = conv_w.shape[1] - 1
    tri = -jnp.tril(jnp.ones((ATT_BLOCK, ATT_BLOCK), BF16))
    tri = jnp.concatenate([tri, tri], axis=0)
    cache_kf = cache_k.reshape(depth * bs, -1, hd)
    cache_vf = cache_v.reshape(depth * bs, -1, hd)

    xp, xs = x_prompt, x_sample
    outs = [[] for _ in range(6)]
    for l in range(depth):
        w_in_bf = w_in[l].astype(BF16)
        n_route = n_groups * (1 + n_exp)
        route_w = jnp.concatenate([w_route_group[l], w_route_expert[l], jnp.zeros((d, LANES - n_route), F32)], axis=1)
        route_b = jnp.concatenate([b_route_group[l], b_route_expert[l], jnp.zeros((LANES - n_route,), F32)])
        w = {
            "cv": w_in_bf[:, :c], "cg": w_in_bf[:, c:2 * c], "q": w_in_bf[:, 2 * c:2 * c + hd],
            "k": w_in_bf[:, 2 * c + hd:2 * c + 2 * hd], "v": w_in_bf[:, 2 * c + 2 * hd:],
            "conv_w": conv_w[l], "conv_b": conv_b[l], "conv_ln_g": conv_ln_g[l], "conv_ln_b": conv_ln_b[l],
            "out_a": w_out[l, :c].astype(BF16), "out_b": w_out[l, c:].astype(BF16),
            "ln1_g": ln1_g[l], "ln1_b": ln1_b[l], "ln2_g": ln2_g[l], "ln2_b": ln2_b[l],
            "route_w": route_w, "route_b": route_b.reshape(1, LANES),
            "exp_gate": w_exp_gate[l].astype(BF16).transpose(0, 2, 1, 3).reshape(n_groups, d, n_exp * f),
            "exp_up": w_exp_up[l].astype(BF16).transpose(0, 2, 1, 3).reshape(n_groups, d, n_exp * f),
            "exp_down": w_exp_down[l].astype(BF16).reshape(n_groups, n_exp * f, d),
            "tri": tri, "n_groups": n_groups, "n_exp": n_exp,
        }
        rows = bp + bs
        pad = (-rows) % 16
        cond = jnp.concatenate([c_prompt, c_sample, jnp.zeros((pad, d), F32)], axis=0)
        mod = _ada(cond, w_ada[l], b_ada[l])[:rows].reshape(rows, 6, d)
        t_s = xs.shape[1]
        hist_p = jnp.zeros((bp, CONV_HALO, c), F32)
        hist_s = jnp.concatenate([jnp.zeros((bs, CONV_HALO - n_hist, c), F32), state_conv[l]], axis=1)
        cache = (cache_kf, cache_vf, l * bs)
        xp, cp, kp, vp = _trunk_layer(xp, mod[:bp], hist_p, None, w, dh, alpha)
        xs, cs, ks, vs = _trunk_layer(xs, mod[bp:], hist_s, cache, w, dh, alpha)
        for lst, val in zip(outs, (kp, vp, cp, ks, vs, cs)):
            lst.append(val)
    return (xp, xs) + tuple(jnp.stack(o) for o in outs)
```

```python
import functools
import math

import jax
import jax.numpy as jnp
from jax import lax
from jax.experimental import pallas as pl
from jax.experimental.pallas import tpu as pltpu

F32 = jnp.float32
BF16 = jnp.bfloat16

LN_EPS = 1e-5
TOP_K_IN_GROUP = 2
LANES = 128
VMEM_LIMIT_BYTES = 56 * 1024 * 1024
ROW_TILE = 256
ATT_BLOCK = 256
ATT_HEADS_PER_STEP = 4
LOG2E = math.log2(math.e)
DEAD_LOG2 = -160.0
CONV_HALO = 32
CONV_ROWS = 32


def _params(*sem):
    return pltpu.CompilerParams(dimension_semantics=sem, vmem_limit_bytes=VMEM_LIMIT_BYTES)


def _layer_norm(x):
    mu = jnp.mean(x, axis=-1, keepdims=True)
    xc = x - mu
    var = jnp.mean(xc * xc, axis=-1, keepdims=True)
    return xc * lax.rsqrt(var + LN_EPS)


def _sigmoid(x):
    return 1.0 / (1.0 + jnp.exp(-x))


def _split_bf16(x):
    hi = x.astype(BF16)
    lo = (x - hi.astype(F32)).astype(BF16)
    return hi, lo


def _row_blocks(b, t):
    if t >= ROW_TILE:
        assert t % ROW_TILE == 0
        return 1, ROW_TILE
    bb = min(b, ROW_TILE // t)
    assert b % bb == 0
    return bb, t


def _ada_kernel(c_ref, w_ref, b_ref, o_ref):
    c = c_ref[...]
    s = c * _sigmoid(c)
    o_ref[...] = jnp.dot(s.astype(BF16), w_ref[...].astype(BF16), preferred_element_type=F32) + b_ref[...]


def _ada(c, w, b):
    r, d = c.shape
    n = w.shape[1]
    tn = 512
    assert n % tn == 0
    return pl.pallas_call(
        _ada_kernel,
        grid=(n // tn,),
        in_specs=[pl.BlockSpec((r, d), lambda j: (0, 0)),
                  pl.BlockSpec((d, tn), lambda j: (0, j)),
                  pl.BlockSpec((1, tn), lambda j: (0, j))],
        out_specs=pl.BlockSpec((r, tn), lambda j: (0, j)),
        out_shape=jax.ShapeDtypeStruct((r, n), F32),
        compiler_params=_params("arbitrary"),
        name="ada",
    )(c, w, b.reshape(1, n))


def _ln_mod_kernel(x_ref, sh_ref, sc_ref, o_ref):
    y = _layer_norm(x_ref[...])
    o_ref[...] = (y * (1.0 + sc_ref[...]) + sh_ref[...]).astype(o_ref.dtype)


def _ln_mod(x, shift, scale):
    b, t, d = x.shape
    bb, tt = _row_blocks(b, t)
    row = pl.BlockSpec((bb, tt, d), lambda i, j: (i, j, 0))
    per_b = pl.BlockSpec((bb, 1, d), lambda i, j: (i, 0, 0))
    return pl.pallas_call(
        _ln_mod_kernel,
        grid=(b // bb, t // tt),
        in_specs=[row, per_b, per_b],
        out_specs=row,
        out_shape=jax.ShapeDtypeStruct((b, t, d), BF16),
        compiler_params=_params("arbitrary", "arbitrary"),
        name="ln_mod",
    )(x, shift, scale)


def _proj_glu_kernel(h_ref, wv_ref, wg_ref, o_ref):
    h = h_ref[...]
    cv = jnp.dot(h, wv_ref[...], preferred_element_type=F32)
    cg = jnp.dot(h, wg_ref[...], preferred_element_type=F32)
    o_ref[...] = cv * _sigmoid(cg)


def _proj_kernel(h_ref, w_ref, *o_refs, scale):
    y = jnp.dot(h_ref[...], w_ref[...], preferred_element_type=F32)
    if scale != 1.0:
        y = y * scale
    for o_ref in o_refs:
        o_ref[...] = y.astype(o_ref.dtype)


def _proj(h, ws, out_dtypes, glu=False, scale=1.0):
    m, d = h.shape
    n = ws[0].shape[1]
    tm = min(m, 512)
    assert m % tm == 0
    w_spec = pl.BlockSpec((d, n), lambda i: (0, 0), pipeline_mode=pl.Buffered(1))
    o_spec = pl.BlockSpec((tm, n), lambda i: (i, 0))
    outs = pl.pallas_call(
        _proj_glu_kernel if glu else functools.partial(_proj_kernel, scale=scale),
        grid=(m // tm,),
        in_specs=[pl.BlockSpec((tm, d), lambda i: (i, 0))] + [w_spec] * len(ws),
        out_specs=[o_spec] * len(out_dtypes),
        out_shape=[jax.ShapeDtypeStruct((m, n), dt) for dt in out_dtypes],
        compiler_params=_params("arbitrary"),
        name="proj_glu" if glu else "proj",
    )(h, *ws)
    return outs


def _conv_kernel(a_ref, halo_ref, hist_ref, cw_ref, cb_ref, g_ref, b_ref, o_ref, s_ref, *, width):
    i = pl.program_id(1)
    tt = a_ref.shape[1]

    @pl.when(i == 0)
    def _():
        s_ref[0:CONV_HALO, :] = hist_ref[0]

    @pl.when(i > 0)
    def _():
        s_ref[0:CONV_HALO, :] = halo_ref[0]

    s_ref[CONV_HALO:CONV_HALO + tt, :] = a_ref[0]
    first = CONV_HALO - (width - 1)
    cw = cw_ref[...]
    for r in range(tt // CONV_ROWS):
        base = r * CONV_ROWS
        acc = jnp.broadcast_to(cb_ref[...], (CONV_ROWS, cw.shape[1]))
        for w in range(width):
            acc = acc + s_ref[base + first + w:base + first + w + CONV_ROWS, :] * cw[w:w + 1, :]
        y = _layer_norm(acc) * g_ref[...] + b_ref[...]
        o_ref[0, base:base + CONV_ROWS, :] = (y * _sigmoid(y)).astype(o_ref.dtype)


def _conv(a, hist, cw, cb, g, bt):
    b, t, c = a.shape
    width = cw.shape[0]
    assert width - 1 <= CONV_HALO
    tt = min(t, ROW_TILE)
    assert t % tt == 0 and tt % CONV_HALO == 0
    per_halo = tt // CONV_HALO
    vec = pl.BlockSpec((1, c), lambda i, j: (0, 0))
    return pl.pallas_call(
        functools.partial(_conv_kernel, width=width),
        grid=(b, t // tt),
        in_specs=[pl.BlockSpec((1, tt, c), lambda i, j: (i, j, 0)),
                  pl.BlockSpec((1, CONV_HALO, c), lambda i, j: (i, jnp.maximum(j * per_halo - 1, 0), 0)),
                  pl.BlockSpec((1, CONV_HALO, c), lambda i, j: (i, 0, 0)),
                  pl.BlockSpec((width, c), lambda i, j: (0, 0)),
                  vec, vec, vec],
        out_specs=pl.BlockSpec((1, tt, c), lambda i, j: (i, j, 0)),
        out_shape=jax.ShapeDtypeStruct((b, t, c), BF16),
        scratch_shapes=[pltpu.VMEM((CONV_HALO + tt, c), F32)],
        compiler_params=_params("arbitrary", "arbitrary"),
        name="conv",
    )(a, a, hist, cw, cb.reshape(1, c), g.reshape(1, c), bt.reshape(1, c))


def _softplus2(z):
    neg_abs = lax.bitcast_convert_type(lax.bitcast_convert_type(z, jnp.uint32) | jnp.uint32(0x80000000), F32)
    return jnp.maximum(z, 0.0) + jnp.log(1.0 + jnp.exp2(neg_abs)) * LOG2E


def _sb_logits(qs, ks):
    nt = (((1,), (1,)), ((), ()))
    return jnp.concatenate([lax.dot_general(q, k, nt, preferred_element_type=F32) for q, k in zip(qs, ks)], axis=0)


def _sb_finish(z, vs, u_neg, carry, causal):
    tq, tk = z.shape[0] // len(vs), z.shape[1]
    sp = _softplus2(z)
    if causal:
        assert tq & (tq - 1) == 0
        keep = lax.broadcasted_iota(jnp.int32, z.shape, 1) < (lax.broadcasted_iota(jnp.int32, z.shape, 0) & (tq - 1))
        sp = jnp.where(keep, sp, 0.0)
    hi, lo = _split_bf16(sp)
    if u_neg.shape[0] == 2 * tk:
        rc = jnp.dot(jnp.concatenate([hi, lo], axis=1), u_neg, preferred_element_type=F32)
    else:
        rc = jnp.dot(hi, u_neg, preferred_element_type=F32) + jnp.dot(lo, u_neg, preferred_element_type=F32)
    a = jnp.exp2(z + rc + carry)
    if causal:
        a = jnp.where(keep, a, 0.0)
    a = a.astype(BF16)
    outs = [jnp.dot(a[c * tq:(c + 1) * tq], v, preferred_element_type=F32) for c, v in enumerate(vs)]
    return outs, carry + rc[:, 0:1]


def _attn_self_kernel(q_ref, k_ref, v_ref, u2_ref, o_ref, acc_ref, carry_ref, *, dh):
    i = pl.program_id(2)
    blk = q_ref.shape[0]
    heads = [slice(h * dh, (h + 1) * dh) for h in range(q_ref.shape[1] // dh)]

    def sweep(block, carry, causal):
        rows = pl.ds(pl.multiple_of(block * blk, blk), blk)
        z = _sb_logits([q_ref[:, c] for c in heads], [k_ref[rows, c] for c in heads])
        return _sb_finish(z, [v_ref[rows, c] for c in heads], u2_ref[...], carry, causal)

    outs, carry = sweep(i, jnp.zeros((len(heads) * blk, 1), F32), True)
    for c, o in zip(heads, outs):
        acc_ref[:, c] = o
    carry_ref[...] = carry

    def live(state):
        j, alive = state
        return jnp.logical_and(j < i, alive > 0)

    def body(state):
        j, _ = state
        outs, carry = sweep(i - 1 - j, carry_ref[...], False)
        for c, o in zip(heads, outs):
            acc_ref[:, c] += o
        carry_ref[...] = carry
        return j + 1, (jnp.max(carry) > DEAD_LOG2).astype(jnp.int32)

    lax.while_loop(live, body, (jnp.int32(0), jnp.int32(1)))
    o_ref[...] = acc_ref[...].astype(o_ref.dtype)


def _attn_self(q, k, v, u2, dh):
    b, t, hd = q.shape
    blk = u2.shape[1]
    n_heads = hd // dh
    per_step = math.gcd(n_heads, ATT_HEADS_PER_STEP)
    assert t % blk == 0 and hd % dh == 0 and dh % LANES == 0
    full = pl.BlockSpec((None, t, per_step * dh), lambda bi, h, i: (bi, 0, h), pipeline_mode=pl.Buffered(1))
    tile = pl.BlockSpec((None, blk, per_step * dh), lambda bi, h, i: (bi, i, h))
    return pl.pallas_call(
        functools.partial(_attn_self_kernel, dh=dh),
        grid=(b, n_heads // per_step, t // blk),
        in_specs=[tile, full, full, pl.BlockSpec(u2.shape, lambda bi, h, i: (0, 0))],
        out_specs=tile,
        out_shape=jax.ShapeDtypeStruct((b, t, hd), BF16),
        scratch_shapes=[pltpu.VMEM((blk, per_step * dh), F32), pltpu.VMEM((per_step * blk, 1), F32)],
        compiler_params=_params("arbitrary", "arbitrary", "arbitrary"),
        name="attn_self",
    )(q, k, v, u2)


def _attn_cached_kernel(q_ref, kn_ref, vn_ref, kc_ref, vc_ref, u2_ref, o_ref, acc_ref, carry_ref, *, dh):
    s = pl.program_id(1)
    t, hd = q_ref.shape
    n_heads = hd // dh
    heads = [slice(h * dh, (h + 1) * dh) for h in range(n_heads)]
    span = kc_ref.shape[0] // n_heads
    blk = u2_ref.shape[1]
    qs = [q_ref[:, c] for c in heads]

    def cached(ref, first, h):
        return ref[pl.ds(first * n_heads + h, blk, stride=n_heads), :].astype(BF16)

    @pl.when(s == 0)
    def _():
        outs, carry = _sb_finish(_sb_logits(qs, [kn_ref[:, c] for c in heads]), [vn_ref[:, c] for c in heads],
                                 u2_ref[0:t, 0:t], jnp.zeros((len(heads) * t, 1), F32), True)
        for c, o in zip(heads, outs):
            acc_ref[:, c] = o
        carry_ref[...] = carry

    carry = carry_ref[...]
    total = None
    for cb in reversed(range(span // blk)):
        outs, carry = _sb_finish(_sb_logits(qs, [cached(kc_ref, cb * blk, h) for h in range(n_heads)]),
                                 [cached(vc_ref, cb * blk, h) for h in range(n_heads)], u2_ref[...], carry, False)
        total = outs if total is None else [x + y for x, y in zip(total, outs)]
    for c, o in zip(heads, total):
        acc_ref[:, c] += o
    carry_ref[...] = carry

    @pl.when(s == pl.num_programs(1) - 1)
    def _():
        o_ref[...] = acc_ref[...].astype(o_ref.dtype)


def _attn_cached(q, kn, vn, kc, vc, u2, dh, first):
    b, t, hd = q.shape
    n_heads = hd // dh
    p = kc.shape[1] // n_heads
    blk = u2.shape[1]
    span = min(p, 2 * blk)
    assert p % span == 0 and span % blk == 0 and t <= blk and t % 16 == 0 and kc.shape[2] == dh == LANES
    n_steps = p // span
    new = pl.BlockSpec((None, t, hd), lambda bi, s: (bi, 0, 0))
    old = pl.BlockSpec((None, span * n_heads, dh), lambda bi, s: (first + bi, n_steps - 1 - s, 0))
    return pl.pallas_call(
        functools.partial(_attn_cached_kernel, dh=dh),
        grid=(b, n_steps),
        in_specs=[new, new, new, old, old, pl.BlockSpec(u2.shape, lambda bi, s: (0, 0))],
        out_specs=new,
        out_shape=jax.ShapeDtypeStruct((b, t, hd), BF16),
        scratch_shapes=[pltpu.VMEM((t, hd), F32), pltpu.VMEM((hd // dh * t, 1), F32)],
        compiler_params=_params("arbitrary", "arbitrary"),
        name="attn_cached",
    )(q, kn, vn, kc, vc, u2)


def _first_max(cols):
    m = cols[0]
    for c in cols[1:]:
        m = jnp.maximum(m, c)
    hot, found = [], None
    for c in cols:
        is_max = c == m
        if found is None:
            hot.append(is_max)
            found = is_max
        else:
            hot.append(jnp.logical_and(is_max, jnp.logical_not(found)))
            found = jnp.logical_or(found, is_max)
    return m, hot


def _route(logits, n_groups, n_exp):
    lg = [logits[:, g:g + 1] for g in range(n_groups)]
    m, g_hot = _first_max(lg)
    denom = jnp.exp(lg[0] - m)
    for c in lg[1:]:
        denom = denom + jnp.exp(c - m)
    p_top = 1.0 / denom
    le = []
    for e in range(n_exp):
        sel = jnp.zeros_like(m)
        for g in range(n_groups):
            col = n_groups + g * n_exp + e
            sel = jnp.where(g_hot[g], logits[:, col:col + 1], sel)
        le.append(sel)
    v1, hot1 = _first_max(le)
    rest = [jnp.where(h, -jnp.inf, c) for h, c in zip(hot1, le)]
    v2, hot2 = _first_max(rest)
    assert TOP_K_IN_GROUP == 2
    e2 = jnp.exp(v2 - v1)
    w1 = p_top / (1.0 + e2)
    w2 = p_top * e2 / (1.0 + e2)
    lane = lax.broadcasted_iota(jnp.int32, logits.shape, 1)
    gates = jnp.zeros(logits.shape, F32)
    for g in range(n_groups):
        for e in range(n_exp):
            val = jnp.where(hot1[e], w1, jnp.where(hot2[e], w2, 0.0))
            val = jnp.where(g_hot[g], val, 0.0)
            gates = jnp.where(lane == g * n_exp + e, val, gates)
    return gates


def _row_chunks(bb, tt):
    if bb % 2 == 0:
        return [(slice(k * bb // 2, (k + 1) * bb // 2), slice(0, tt)) for k in range(2)]
    assert tt % 32 == 0
    return [(slice(0, bb), slice(k * tt // 2, (k + 1) * tt // 2)) for k in range(2)]


def _out_kernel(co_ref, at_ref, w_ref, x_ref, g1_ref, sc2_ref, sh2_ref, lg_ref, lb_ref, wr_ref, br_ref,
                x1_ref, h2_ref, gt_ref, *, alpha, n_groups, n_exp):
    bb, tt, d = x_ref.shape
    w_hi, w_lo = _split_bf16(wr_ref[...])
    for bs, ts in _row_chunks(bb, tt):
        shape = x_ref[bs, ts, :].shape
        rows = shape[0] * shape[1]
        mixed = jnp.concatenate([co_ref[bs, ts, :], at_ref[bs, ts, :]], axis=2).reshape(rows, w_ref.shape[0])
        mix = jnp.dot(mixed, w_ref[...], preferred_element_type=F32).reshape(shape)
        x1 = _layer_norm(alpha * x_ref[bs, ts, :] + (1.0 + g1_ref[bs]) * mix) * lg_ref[...] + lb_ref[...]
        x1_ref[bs, ts, :] = x1
        h2 = _layer_norm(x1) * (1.0 + sc2_ref[bs]) + sh2_ref[bs]
        h2_ref[bs, ts, :] = h2.astype(h2_ref.dtype)
        h_hi, h_lo = _split_bf16(h2.reshape(rows, d))
        logits = (jnp.dot(h_hi, w_hi, preferred_element_type=F32) + jnp.dot(h_lo, w_hi, preferred_element_type=F32)
                  + jnp.dot(h_hi, w_lo, preferred_element_type=F32)) + br_ref[...]
        gt_ref[bs, ts, :] = _route(logits, n_groups, n_exp).reshape(shape[0], shape[1], LANES)


def _out_proj(co, at, w, x, gate1, scale2, shift2, ln_g, ln_b, w_route, b_route, alpha, n_groups, n_exp):
    b, t, d = x.shape
    bb, tt = _row_blocks(b, t)
    row = lambda n: pl.BlockSpec((bb, tt, n), lambda i, j: (i, j, 0))
    per_b = pl.BlockSpec((bb, 1, d), lambda i, j: (i, 0, 0))
    const = lambda r, n: pl.BlockSpec((r, n), lambda i, j: (0, 0), pipeline_mode=pl.Buffered(1))
    return pl.pallas_call(
        functools.partial(_out_kernel, alpha=alpha, n_groups=n_groups, n_exp=n_exp),
        grid=(b // bb, t // tt),
        in_specs=[row(co.shape[2]), row(at.shape[2]), const(*w.shape), row(d), per_b, per_b, per_b,
                  const(1, d), const(1, d), const(d, LANES), const(1, LANES)],
        out_specs=[row(d), row(d), row(LANES)],
        out_shape=[jax.ShapeDtypeStruct((b, t, d), F32), jax.ShapeDtypeStruct((b, t, d), BF16),
                   jax.ShapeDtypeStruct((b, t, LANES), F32)],
        compiler_params=_params("arbitrary", "arbitrary"),
        name="out_proj",
    )(co, at, w, x, gate1, scale2, shift2, ln_g.reshape(1, d), ln_b.reshape(1, d), w_route, b_route)


def _moe_act_kernel(h_ref, gt_ref, wg_ref, wu_ref, o_ref, *, n_exp, f):
    h = h_ref[...]
    gates = gt_ref[...]
    chunk = n_exp * f
    for g in range(wg_ref.shape[1] // chunk):
        cols = slice(g * chunk, (g + 1) * chunk)
        a = jnp.dot(h, wg_ref[:, cols], preferred_element_type=F32)
        up = jnp.dot(h, wu_ref[:, cols], preferred_element_type=F32)
        parts = []
        for e in range(n_exp):
            ae = a[:, e * f:(e + 1) * f]
            gate = gates[:, g * n_exp + e:g * n_exp + e + 1]
            parts.append(ae * _sigmoid(ae) * up[:, e * f:(e + 1) * f] * gate)
        o_ref[:, cols] = jnp.concatenate(parts, axis=1).astype(o_ref.dtype)


def _moe_act(h2, gates, wg, wu, n_exp, f):
    m, d = h2.shape
    n = wg.shape[1]
    tm = min(m, 512)
    assert m % tm == 0
    w_spec = pl.BlockSpec((d, n), lambda i: (0, 0), pipeline_mode=pl.Buffered(1))
    assert n % (n_exp * f) == 0 and n // f <= LANES
    return pl.pallas_call(
        functools.partial(_moe_act_kernel, n_exp=n_exp, f=f),
        grid=(m // tm,),
        in_specs=[pl.BlockSpec((tm, d), lambda i: (i, 0)), pl.BlockSpec((tm, LANES), lambda i: (i, 0)), w_spec, w_spec],
        out_specs=pl.BlockSpec((tm, n), lambda i: (i, 0)),
        out_shape=jax.ShapeDtypeStruct((m, n), BF16),
        compiler_params=_params("arbitrary"),
        name="moe_act",
    )(h2, gates, wg, wu)


def _moe_out_kernel(a_ref, w_ref, x1_ref, g2_ref, lg_ref, lb_ref, y_ref, *, alpha):
    bb, tt, d = x1_ref.shape
    for bs, ts in _row_chunks(bb, tt):
        shape = x1_ref[bs, ts, :].shape
        act = a_ref[bs, ts, :].reshape(shape[0] * shape[1], a_ref.shape[2])
        ff = jnp.dot(act, w_ref[...], preferred_element_type=F32).reshape(shape)
        y_ref[bs, ts, :] = _layer_norm(alpha * x1_ref[bs, ts, :] + (1.0 + g2_ref[bs]) * ff) * lg_ref[...] + lb_ref[...]


def _moe_out(act, wd, x1, gate2, ln_g, ln_b, alpha):
    b, t, d = x1.shape
    bb, tt = _row_blocks(b, t)
    row = lambda n: pl.BlockSpec((bb, tt, n), lambda i, j: (i, j, 0))
    per_b = pl.BlockSpec((bb, 1, d), lambda i, j: (i, 0, 0))
    const = lambda r, n: pl.BlockSpec((r, n), lambda i, j: (0, 0), pipeline_mode=pl.Buffered(1))
    return pl.pallas_call(
        functools.partial(_moe_out_kernel, alpha=alpha),
        grid=(b // bb, t // tt),
        in_specs=[row(act.shape[2]), const(*wd.shape), row(d), per_b, const(1, d), const(1, d)],
        out_specs=row(d),
        out_shape=jax.ShapeDtypeStruct((b, t, d), F32),
        compiler_params=_params("arbitrary", "arbitrary"),
        name="moe_out",
    )(act, wd, x1, gate2, ln_g.reshape(1, d), ln_b.reshape(1, d))


def _trunk_layer(x, mod, hist, cache, w, dh, alpha):
    b, t, d = x.shape
    m = b * t
    mods = [mod[:, i:i + 1] for i in range(6)]
    shift1, scale1, gate1, shift2, scale2, gate2 = mods
    h1 = _ln_mod(x, shift1, scale1).reshape(m, d)
    (a_glu,) = _proj(h1, (w["cv"], w["cg"]), (F32,), glu=True)
    (q,) = _proj(h1, (w["q"],), (BF16,), scale=dh ** -0.5 * LOG2E)
    k, k_bf = _proj(h1, (w["k"],), (F32, BF16))
    v, v_bf = _proj(h1, (w["v"],), (F32, BF16))
    c = a_glu.shape[1]
    hd = q.shape[1]
    a_glu = a_glu.reshape(b, t, c)
    conv_out = _conv(a_glu, hist, w["conv_w"], w["conv_b"], w["conv_ln_g"], w["conv_ln_b"])
    q3, k3, v3 = (z.reshape(b, t, hd) for z in (q, k_bf, v_bf))
    if cache is None:
        attn = _attn_self(q3, k3, v3, w["tri"], dh)
    else:
        attn = _attn_cached(q3, k3, v3, cache[0], cache[1], w["tri"], dh, cache[2])
    x1, h2, gates = _out_proj(conv_out, attn, w["out"], x, gate1, scale2, shift2, w["ln1_g"],
                              w["ln1_b"], w["route_w"], w["route_b"], alpha, w["n_groups"], w["n_exp"])
    act = _moe_act(h2.reshape(m, d), gates.reshape(m, LANES), w["exp_gate"], w["exp_up"], w["n_exp"], w["f"])
    y = _moe_out(act.reshape(b, t, -1), w["exp_down"], x1, gate2, w["ln2_g"], w["ln2_b"], alpha)
    n_hist = w["conv_w"].shape[0] - 1
    assert t >= n_hist
    conv_state = a_glu[:, t - n_hist:]
    n_heads = hd // dh
    return y, conv_state, k.reshape(b, t, n_heads, dh), v.reshape(b, t, n_heads, dh)


def kernel(x_prompt, x_sample, cache_k, cache_v, state_conv, c_prompt, c_sample, w_ada, b_ada, w_in, conv_w, conv_b, conv_ln_g, conv_ln_b, w_out, ln1_g, ln1_b, w_route_group, b_route_group, w_route_expert, b_route_expert, w_exp_gate, w_exp_up, w_exp_down, ln2_g, ln2_b):
    depth = w_ada.shape[0]
    d = x_prompt.shape[2]
    bp = x_prompt.shape[0]
    bs = x_sample.shape[0]
    dh = cache_k.shape[4]
    hd = cache_k.shape[3] * dh
    c = conv_w.shape[2]
    n_groups, n_exp, _, f = w_exp_gate.shape[1:]
    alpha = (2.0 * depth) ** 0.25
    n_hist = conv_w.shape[1] - 1
    tri = -jnp.tril(jnp.ones((ATT_BLOCK, ATT_BLOCK), BF16))
    tri = jnp.concatenate([tri, tri], axis=0)
    cache_kf = cache_k.reshape(depth * bs, -1, dh)
    cache_vf = cache_v.reshape(depth * bs, -1, dh)

    xp, xs = x_prompt, x_sample
    outs = [[] for _ in range(6)]
    for l in range(depth):
        w_in_bf = w_in[l].astype(BF16)
        n_route = n_groups * (1 + n_exp)
        route_w = jnp.concatenate([w_route_group[l], w_route_expert[l], jnp.zeros((d, LANES - n_route), F32)], axis=1)
        route_b = jnp.concatenate([b_route_group[l], b_route_expert[l], jnp.zeros((LANES - n_route,), F32)])
        w = {
            "cv": w_in_bf[:, :c], "cg": w_in_bf[:, c:2 * c], "q": w_in_bf[:, 2 * c:2 * c + hd],
            "k": w_in_bf[:, 2 * c + hd:2 * c + 2 * hd], "v": w_in_bf[:, 2 * c + 2 * hd:],
            "conv_w": conv_w[l], "conv_b": conv_b[l], "conv_ln_g": conv_ln_g[l], "conv_ln_b": conv_ln_b[l],
            "out": w_out[l].astype(BF16),
            "ln1_g": ln1_g[l], "ln1_b": ln1_b[l], "ln2_g": ln2_g[l], "ln2_b": ln2_b[l],
            "route_w": route_w, "route_b": route_b.reshape(1, LANES),
            "exp_gate": w_exp_gate[l].astype(BF16).transpose(2, 0, 1, 3).reshape(d, n_groups * n_exp * f),
            "exp_up": w_exp_up[l].astype(BF16).transpose(2, 0, 1, 3).reshape(d, n_groups * n_exp * f),
            "exp_down": w_exp_down[l].astype(BF16).reshape(n_groups * n_exp * f, d),
            "tri": tri, "n_groups": n_groups, "n_exp": n_exp, "f": f,
        }
        rows = bp + bs
        pad = (-rows) % 16
        cond = jnp.concatenate([c_prompt, c_sample, jnp.zeros((pad, d), F32)], axis=0)
        mod = _ada(cond, w_ada[l], b_ada[l])[:rows].reshape(rows, 6, d)
        t_s = xs.shape[1]
        hist_p = jnp.zeros((bp, CONV_HALO, c), F32)
        hist_s = jnp.concatenate([jnp.zeros((bs, CONV_HALO - n_hist, c), F32), state_conv[l]], axis=1)
        cache = (cache_kf, cache_vf, l * bs)
        xp, cp, kp, vp = _trunk_layer(xp, mod[:bp], hist_p, None, w, dh, alpha)
        xs, cs, ks, vs = _trunk_layer(xs, mod[bp:], hist_s, cache, w, dh, alpha)
        for lst, val in zip(outs, (kp, vp, cp, ks, vs, cs)):
            lst.append(val)
    return (xp, xs) + tuple(jnp.stack(o) for o in outs)
```

```python
import functools
import math

import jax
import jax.numpy as jnp
from jax import lax
from jax.experimental import pallas as pl
from jax.experimental.pallas import tpu as pltpu

F32 = jnp.float32
BF16 = jnp.bfloat16

LN_EPS = 1e-5
TOP_K_IN_GROUP = 2
LANES = 128
VMEM_LIMIT_BYTES = 56 * 1024 * 1024
ROW_TILE = 256
ATT_BLOCK = 256
ATT_HEADS_PER_STEP = 4
LOG2E = math.log2(math.e)
DEAD_LOG2 = -160.0
CONV_HALO = 32
CONV_ROWS = 32


def _params(*sem):
    return pltpu.CompilerParams(dimension_semantics=sem, vmem_limit_bytes=VMEM_LIMIT_BYTES)


def _layer_norm(x):
    mu = jnp.mean(x, axis=-1, keepdims=True)
    xc = x - mu
    var = jnp.mean(xc * xc, axis=-1, keepdims=True)
    return xc * lax.rsqrt(var + LN_EPS)


def _sigmoid(x):
    return 1.0 / (1.0 + jnp.exp(-x))


def _split_bf16(x):
    hi = x.astype(BF16)
    lo = (x - hi.astype(F32)).astype(BF16)
    return hi, lo


def _row_blocks(b, t):
    if t >= ROW_TILE:
        assert t % ROW_TILE == 0
        return 1, ROW_TILE
    bb = min(b, ROW_TILE // t)
    assert b % bb == 0
    return bb, t


def _ada_kernel(c_ref, w_ref, b_ref, o_ref):
    c = c_ref[...]
    s = c * _sigmoid(c)
    o_ref[...] = jnp.dot(s.astype(BF16), w_ref[...].astype(BF16), preferred_element_type=F32) + b_ref[...]


def _ada(c, w, b):
    r, d = c.shape
    n = w.shape[1]
    tn = 512
    assert n % tn == 0
    return pl.pallas_call(
        _ada_kernel,
        grid=(n // tn,),
        in_specs=[pl.BlockSpec((r, d), lambda j: (0, 0)),
                  pl.BlockSpec((d, tn), lambda j: (0, j)),
                  pl.BlockSpec((1, tn), lambda j: (0, j))],
        out_specs=pl.BlockSpec((r, tn), lambda j: (0, j)),
        out_shape=jax.ShapeDtypeStruct((r, n), F32),
        compiler_params=_params("arbitrary"),
        name="ada",
    )(c, w, b.reshape(1, n))


def _ln_mod_kernel(x_ref, sh_ref, sc_ref, o_ref):
    y = _layer_norm(x_ref[...])
    o_ref[...] = (y * (1.0 + sc_ref[...]) + sh_ref[...]).astype(o_ref.dtype)


def _ln_mod(x, shift, scale):
    b, t, d = x.shape
    bb, tt = _row_blocks(b, t)
    row = pl.BlockSpec((bb, tt, d), lambda i, j: (i, j, 0))
    per_b = pl.BlockSpec((bb, 1, d), lambda i, j: (i, 0, 0))
    return pl.pallas_call(
        _ln_mod_kernel,
        grid=(b // bb, t // tt),
        in_specs=[row, per_b, per_b],
        out_specs=row,
        out_shape=jax.ShapeDtypeStruct((b, t, d), BF16),
        compiler_params=_params("arbitrary", "arbitrary"),
        name="ln_mod",
    )(x, shift, scale)


def _proj_glu_kernel(h_ref, wv_ref, wg_ref, o_ref):
    h = h_ref[...]
    cv = jnp.dot(h, wv_ref[...], preferred_element_type=F32)
    cg = jnp.dot(h, wg_ref[...], preferred_element_type=F32)
    o_ref[...] = cv * _sigmoid(cg)


def _proj_kernel(h_ref, w_ref, *o_refs, scale):
    y = jnp.dot(h_ref[...], w_ref[...], preferred_element_type=F32)
    if scale != 1.0:
        y = y * scale
    for o_ref in o_refs:
        o_ref[...] = y.astype(o_ref.dtype)


def _proj(h, ws, out_dtypes, glu=False, scale=1.0):
    m, d = h.shape
    n = ws[0].shape[1]
    tm = min(m, 512)
    assert m % tm == 0
    w_spec = pl.BlockSpec((d, n), lambda i: (0, 0), pipeline_mode=pl.Buffered(1))
    o_spec = pl.BlockSpec((tm, n), lambda i: (i, 0))
    outs = pl.pallas_call(
        _proj_glu_kernel if glu else functools.partial(_proj_kernel, scale=scale),
        grid=(m // tm,),
        in_specs=[pl.BlockSpec((tm, d), lambda i: (i, 0))] + [w_spec] * len(ws),
        out_specs=[o_spec] * len(out_dtypes),
        out_shape=[jax.ShapeDtypeStruct((m, n), dt) for dt in out_dtypes],
        compiler_params=_params("arbitrary"),
        name="proj_glu" if glu else "proj",
    )(h, *ws)
    return outs


def _conv_kernel(a_ref, halo_ref, hist_ref, cw_ref, cb_ref, g_ref, b_ref, o_ref, s_ref, *, width):
    i = pl.program_id(1)
    tt = a_ref.shape[1]

    @pl.when(i == 0)
    def _():
        s_ref[0:CONV_HALO, :] = hist_ref[0]

    @pl.when(i > 0)
    def _():
        s_ref[0:CONV_HALO, :] = halo_ref[0]

    s_ref[CONV_HALO:CONV_HALO + tt, :] = a_ref[0]
    first = CONV_HALO - (width - 1)
    n_ch = cw_ref.shape[1]
    lane_chunk = min(n_ch, 2 * LANES)
    for r in range(tt // CONV_ROWS):
        base = r * CONV_ROWS
        pieces = []
        for lc in range(n_ch // lane_chunk):
            cols = slice(lc * lane_chunk, (lc + 1) * lane_chunk)
            acc = jnp.broadcast_to(cb_ref[:, cols], (CONV_ROWS, lane_chunk))
            for rem in range(8):
                taps = [w for w in range(width) if (first + w) % 8 == rem]
                if not taps:
                    continue
                n_rows = CONV_ROWS + (8 if rem else 0)
                part = None
                for w in taps:
                    start = base + first + w - rem
                    term = s_ref[start:start + n_rows, cols] * cw_ref[w:w + 1, cols]
                    part = term if part is None else part + term
                acc = acc + part[rem:rem + CONV_ROWS]
            pieces.append(acc)
        acc = jnp.concatenate(pieces, axis=1)
        y = _layer_norm(acc) * g_ref[...] + b_ref[...]
        o_ref[0, base:base + CONV_ROWS, :] = (y * _sigmoid(y)).astype(o_ref.dtype)


def _conv(a, hist, cw, cb, g, bt):
    b, t, c = a.shape
    width = cw.shape[0]
    assert width - 1 <= CONV_HALO
    tt = min(t, ROW_TILE)
    assert t % tt == 0 and tt % CONV_HALO == 0
    per_halo = tt // CONV_HALO
    vec = pl.BlockSpec((1, c), lambda i, j: (0, 0))
    return pl.pallas_call(
        functools.partial(_conv_kernel, width=width),
        grid=(b, t // tt),
        in_specs=[pl.BlockSpec((1, tt, c), lambda i, j: (i, j, 0)),
                  pl.BlockSpec((1, CONV_HALO, c), lambda i, j: (i, jnp.maximum(j * per_halo - 1, 0), 0)),
                  pl.BlockSpec((1, CONV_HALO, c), lambda i, j: (i, 0, 0)),
                  pl.BlockSpec((width, c), lambda i, j: (0, 0)),
                  vec, vec, vec],
        out_specs=pl.BlockSpec((1, tt, c), lambda i, j: (i, j, 0)),
        out_shape=jax.ShapeDtypeStruct((b, t, c), BF16),
        scratch_shapes=[pltpu.VMEM((CONV_HALO + tt, c), F32)],
        compiler_params=_params("arbitrary", "arbitrary"),
        name="conv",
    )(a, a, hist, cw, cb.reshape(1, c), g.reshape(1, c), bt.reshape(1, c))


def _softplus2(z):
    neg_abs = lax.bitcast_convert_type(lax.bitcast_convert_type(z, jnp.uint32) | jnp.uint32(0x80000000), F32)
    return jnp.maximum(z, 0.0) + jnp.log(1.0 + jnp.exp2(neg_abs)) * LOG2E


def _sb_logits(qs, ks):
    nt = (((1,), (1,)), ((), ()))
    return jnp.concatenate([lax.dot_general(q, k, nt, preferred_element_type=F32) for q, k in zip(qs, ks)], axis=0)


def _sb_finish(z, vs, u_neg, carry, causal):
    tq, tk = z.shape[0] // len(vs), z.shape[1]
    sp = _softplus2(z)
    if causal:
        assert tq & (tq - 1) == 0
        keep = lax.broadcasted_iota(jnp.int32, z.shape, 1) < (lax.broadcasted_iota(jnp.int32, z.shape, 0) & (tq - 1))
        sp = jnp.where(keep, sp, 0.0)
    hi, lo = _split_bf16(sp)
    if u_neg.shape[0] == 2 * tk:
        rc = jnp.dot(jnp.concatenate([hi, lo], axis=1), u_neg, preferred_element_type=F32)
    else:
        rc = jnp.dot(hi, u_neg, preferred_element_type=F32) + jnp.dot(lo, u_neg, preferred_element_type=F32)
    a = jnp.exp2(z + rc + carry)
    if causal:
        a = jnp.where(keep, a, 0.0)
    a = a.astype(BF16)
    outs = [jnp.dot(a[c * tq:(c + 1) * tq], v, preferred_element_type=F32) for c, v in enumerate(vs)]
    return outs, carry + rc[:, 0:1]


def _attn_self_kernel(q_ref, k_ref, v_ref, u2_ref, o_ref, acc_ref, carry_ref, *, dh):
    i = pl.program_id(2)
    blk = q_ref.shape[0]
    heads = [slice(h * dh, (h + 1) * dh) for h in range(q_ref.shape[1] // dh)]

    def sweep(block, carry, causal):
        rows = pl.ds(pl.multiple_of(block * blk, blk), blk)
        z = _sb_logits([q_ref[:, c] for c in heads], [k_ref[rows, c] for c in heads])
        return _sb_finish(z, [v_ref[rows, c] for c in heads], u2_ref[...], carry, causal)

    outs, carry = sweep(i, jnp.zeros((len(heads) * blk, 1), F32), True)
    for c, o in zip(heads, outs):
        acc_ref[:, c] = o
    carry_ref[...] = carry

    def live(state):
        j, alive = state
        return jnp.logical_and(j < i, alive > 0)

    def body(state):
        j, _ = state
        outs, carry = sweep(i - 1 - j, carry_ref[...], False)
        for c, o in zip(heads, outs):
            acc_ref[:, c] += o
        carry_ref[...] = carry
        return j + 1, (jnp.max(carry) > DEAD_LOG2).astype(jnp.int32)

    lax.while_loop(live, body, (jnp.int32(0), jnp.int32(1)))
    o_ref[...] = acc_ref[...].astype(o_ref.dtype)


def _attn_self(q, k, v, u2, dh):
    b, t, hd = q.shape
    blk = u2.shape[1]
    n_heads = hd // dh
    per_step = math.gcd(n_heads, ATT_HEADS_PER_STEP)
    assert t % blk == 0 and hd % dh == 0 and dh % LANES == 0
    full = pl.BlockSpec((None, t, per_step * dh), lambda bi, h, i: (bi, 0, h), pipeline_mode=pl.Buffered(1))
    tile = pl.BlockSpec((None, blk, per_step * dh), lambda bi, h, i: (bi, i, h))
    return pl.pallas_call(
        functools.partial(_attn_self_kernel, dh=dh),
        grid=(b, n_heads // per_step, t // blk),
        in_specs=[tile, full, full, pl.BlockSpec(u2.shape, lambda bi, h, i: (0, 0))],
        out_specs=tile,
        out_shape=jax.ShapeDtypeStruct((b, t, hd), BF16),
        scratch_shapes=[pltpu.VMEM((blk, per_step * dh), F32), pltpu.VMEM((per_step * blk, 1), F32)],
        compiler_params=_params("arbitrary", "arbitrary", "arbitrary"),
        name="attn_self",
    )(q, k, v, u2)


def _attn_cached_kernel(q_ref, kn_ref, vn_ref, kc_ref, vc_ref, u2_ref, o_ref, acc_ref, carry_ref, alive_ref, *,
                        dh):
    s = pl.program_id(1)
    t, hd = q_ref.shape
    n_heads = hd // dh
    heads = [slice(h * dh, (h + 1) * dh) for h in range(n_heads)]
    span = kc_ref.shape[0] // n_heads
    blk = u2_ref.shape[1]
    qs = [q_ref[:, c] for c in heads]

    def cached(ref, first, h):
        return ref[pl.ds(first * n_heads + h, blk, stride=n_heads), :].astype(BF16)

    @pl.when(s == 0)
    def _():
        outs, carry = _sb_finish(_sb_logits(qs, [kn_ref[:, c] for c in heads]), [vn_ref[:, c] for c in heads],
                                 u2_ref[0:t, 0:t], jnp.zeros((len(heads) * t, 1), F32), True)
        for c, o in zip(heads, outs):
            acc_ref[:, c] = o
        carry_ref[...] = carry
        alive_ref[0] = (jnp.max(carry) > DEAD_LOG2).astype(jnp.int32)

    @pl.when(alive_ref[0] > 0)
    def _():
        carry = carry_ref[...]
        total = None
        for cb in reversed(range(span // blk)):
            outs, carry = _sb_finish(_sb_logits(qs, [cached(kc_ref, cb * blk, h) for h in range(n_heads)]),
                                     [cached(vc_ref, cb * blk, h) for h in range(n_heads)], u2_ref[...], carry,
                                     False)
            total = outs if total is None else [x + y for x, y in zip(total, outs)]
        for c, o in zip(heads, total):
            acc_ref[:, c] += o
        carry_ref[...] = carry
        alive_ref[0] = (jnp.max(carry) > DEAD_LOG2).astype(jnp.int32)

    @pl.when(s == pl.num_programs(1) - 1)
    def _():
        o_ref[...] = acc_ref[...].astype(o_ref.dtype)


def _attn_cached(q, kn, vn, kc, vc, u2, dh, first):
    b, t, hd = q.shape
    n_heads = hd // dh
    p = kc.shape[1] // n_heads
    blk = u2.shape[1]
    span = min(p, 2 * blk)
    assert p % span == 0 and span % blk == 0 and t <= blk and t % 16 == 0 and kc.shape[2] == dh == LANES
    n_steps = p // span
    new = pl.BlockSpec((None, t, hd), lambda bi, s: (bi, 0, 0))
    old = pl.BlockSpec((None, span * n_heads, dh), lambda bi, s: (first + bi, n_steps - 1 - s, 0))
    return pl.pallas_call(
        functools.partial(_attn_cached_kernel, dh=dh),
        grid=(b, n_steps),
        in_specs=[new, new, new, old, old, pl.BlockSpec(u2.shape, lambda bi, s: (0, 0))],
        out_specs=new,
        out_shape=jax.ShapeDtypeStruct((b, t, hd), BF16),
        scratch_shapes=[pltpu.VMEM((t, hd), F32), pltpu.VMEM((hd // dh * t, 1), F32), pltpu.SMEM((1,), jnp.int32)],
        compiler_params=_params("arbitrary", "arbitrary"),
        name="attn_cached",
    )(q, kn, vn, kc, vc, u2)


def _first_max(cols):
    m = cols[0]
    for c in cols[1:]:
        m = jnp.maximum(m, c)
    hot, found = [], None
    for c in cols:
        is_max = c == m
        if found is None:
            hot.append(is_max)
            found = is_max
        else:
            hot.append(jnp.logical_and(is_max, jnp.logical_not(found)))
            found = jnp.logical_or(found, is_max)
    return m, hot


def _route(logits, n_groups, n_exp):
    lg = [logits[:, g:g + 1] for g in range(n_groups)]
    m, g_hot = _first_max(lg)
    denom = jnp.exp(lg[0] - m)
    for c in lg[1:]:
        denom = denom + jnp.exp(c - m)
    p_top = 1.0 / denom
    le = []
    for e in range(n_exp):
        sel = jnp.zeros_like(m)
        for g in range(n_groups):
            col = n_groups + g * n_exp + e
            sel = jnp.where(g_hot[g], logits[:, col:col + 1], sel)
        le.append(sel)
    v1, hot1 = _first_max(le)
    rest = [jnp.where(h, -jnp.inf, c) for h, c in zip(hot1, le)]
    v2, hot2 = _first_max(rest)
    assert TOP_K_IN_GROUP == 2
    e2 = jnp.exp(v2 - v1)
    w1 = p_top / (1.0 + e2)
    w2 = p_top * e2 / (1.0 + e2)
    lane = lax.broadcasted_iota(jnp.int32, logits.shape, 1)
    gates = jnp.zeros(logits.shape, F32)
    for g in range(n_groups):
        for e in range(n_exp):
            val = jnp.where(hot1[e], w1, jnp.where(hot2[e], w2, 0.0))
            val = jnp.where(g_hot[g], val, 0.0)
            gates = jnp.where(lane == g * n_exp + e, val, gates)
    return gates


def _row_chunks(bb, tt):
    if bb % 2 == 0:
        return [(slice(k * bb // 2, (k + 1) * bb // 2), slice(0, tt)) for k in range(2)]
    assert tt % 32 == 0
    return [(slice(0, bb), slice(k * tt // 2, (k + 1) * tt // 2)) for k in range(2)]


def _out_kernel(co_ref, at_ref, w_ref, x_ref, g1_ref, sc2_ref, sh2_ref, lg_ref, lb_ref, wr_ref, br_ref,
                x1_ref, h2_ref, gt_ref, *, alpha, n_groups, n_exp):
    bb, tt, d = x_ref.shape
    w_hi, w_lo = _split_bf16(wr_ref[...])
    for bs, ts in _row_chunks(bb, tt):
        shape = x_ref[bs, ts, :].shape
        rows = shape[0] * shape[1]
        mixed = jnp.concatenate([co_ref[bs, ts, :], at_ref[bs, ts, :]], axis=2).reshape(rows, w_ref.shape[0])
        mix = jnp.dot(mixed, w_ref[...], preferred_element_type=F32).reshape(shape)
        x1 = _layer_norm(alpha * x_ref[bs, ts, :] + (1.0 + g1_ref[bs]) * mix) * lg_ref[...] + lb_ref[...]
        x1_ref[bs, ts, :] = x1
        h2 = _layer_norm(x1) * (1.0 + sc2_ref[bs]) + sh2_ref[bs]
        h2_ref[bs, ts, :] = h2.astype(h2_ref.dtype)
        h_hi, h_lo = _split_bf16(h2.reshape(rows, d))
        both = jnp.dot(h_hi, jnp.concatenate([w_hi, w_lo], axis=1), preferred_element_type=F32)
        logits = (both[:, :LANES] + both[:, LANES:] + jnp.dot(h_lo, w_hi, preferred_element_type=F32)) + br_ref[...]
        gt_ref[bs, ts, :] = _route(logits, n_groups, n_exp).reshape(shape[0], shape[1], LANES)


def _out_proj(co, at, w, x, gate1, scale2, shift2, ln_g, ln_b, w_route, b_route, alpha, n_groups, n_exp):
    b, t, d = x.shape
    bb, tt = _row_blocks(b, t)
    row = lambda n: pl.BlockSpec((bb, tt, n), lambda i, j: (i, j, 0))
    per_b = pl.BlockSpec((bb, 1, d), lambda i, j: (i, 0, 0))
    const = lambda r, n: pl.BlockSpec((r, n), lambda i, j: (0, 0), pipeline_mode=pl.Buffered(1))
    return pl.pallas_call(
        functools.partial(_out_kernel, alpha=alpha, n_groups=n_groups, n_exp=n_exp),
        grid=(b // bb, t // tt),
        in_specs=[row(co.shape[2]), row(at.shape[2]), const(*w.shape), row(d), per_b, per_b, per_b,
                  const(1, d), const(1, d), const(d, LANES), const(1, LANES)],
        out_specs=[row(d), row(d), row(LANES)],
        out_shape=[jax.ShapeDtypeStruct((b, t, d), F32), jax.ShapeDtypeStruct((b, t, d), BF16),
                   jax.ShapeDtypeStruct((b, t, LANES), F32)],
        compiler_params=_params("arbitrary", "arbitrary"),
        name="out_proj",
    )(co, at, w, x, gate1, scale2, shift2, ln_g.reshape(1, d), ln_b.reshape(1, d), w_route, b_route)


def _moe_act_kernel(h_ref, gt_ref, wg_ref, wu_ref, o_ref, *, n_exp, f):
    h = h_ref[...]
    gates = gt_ref[...]
    chunk = n_exp * f
    for g in range(wg_ref.shape[1] // chunk):
        cols = slice(g * chunk, (g + 1) * chunk)
        a = jnp.dot(h, wg_ref[:, cols], preferred_element_type=F32)
        up = jnp.dot(h, wu_ref[:, cols], preferred_element_type=F32)
        parts = []
        for e in range(n_exp):
            ae = a[:, e * f:(e + 1) * f]
            gate = gates[:, g * n_exp + e:g * n_exp + e + 1]
            parts.append(ae * _sigmoid(ae) * up[:, e * f:(e + 1) * f] * gate)
        o_ref[:, cols] = jnp.concatenate(parts, axis=1).astype(o_ref.dtype)


def _moe_act(h2, gates, wg, wu, n_exp, f):
    m, d = h2.shape
    n = wg.shape[1]
    tm = min(m, 512)
    assert m % tm == 0
    w_spec = pl.BlockSpec((d, n), lambda i: (0, 0), pipeline_mode=pl.Buffered(1))
    assert n % (n_exp * f) == 0 and n // f <= LANES
    return pl.pallas_call(
        functools.partial(_moe_act_kernel, n_exp=n_exp, f=f),
        grid=(m // tm,),
        in_specs=[pl.BlockSpec((tm, d), lambda i: (i, 0)), pl.BlockSpec((tm, LANES), lambda i: (i, 0)), w_spec, w_spec],
        out_specs=pl.BlockSpec((tm, n), lambda i: (i, 0)),
        out_shape=jax.ShapeDtypeStruct((m, n), BF16),
        compiler_params=_params("arbitrary"),
        name="moe_act",
    )(h2, gates, wg, wu)


def _moe_out_kernel(a_ref, w_ref, x1_ref, g2_ref, lg_ref, lb_ref, y_ref, *, alpha):
    bb, tt, d = x1_ref.shape
    for bs, ts in _row_chunks(bb, tt):
        shape = x1_ref[bs, ts, :].shape
        act = a_ref[bs, ts, :].reshape(shape[0] * shape[1], a_ref.shape[2])
        ff = jnp.dot(act, w_ref[...], preferred_element_type=F32).reshape(shape)
        y_ref[bs, ts, :] =_layer_norm(alpha * x1_ref[bs, ts, :] + (1.0 + g2_ref[bs]) * ff) * lg_ref[...] + lb_ref[...]


def _moe_out(act, wd, x1, gate2, ln_g, ln_b, alpha):
    b, t, d = x1.shape
    bb, tt = _row_blocks(b, t)
    row = lambda n: pl.BlockSpec((bb, tt, n), lambda i, j: (i, j, 0))
    per_b = pl.BlockSpec((bb, 1, d), lambda i, j: (i, 0, 0))
    const = lambda r, n: pl.BlockSpec((r, n), lambda i, j: (0, 0), pipeline_mode=pl.Buffered(1))
    return pl.pallas_call(
        functools.partial(_moe_out_kernel, alpha=alpha),
        grid=(b // bb, t // tt),
        in_specs=[row(act.shape[2]), const(*wd.shape), row(d), per_b, const(1, d), const(1, d)],
        out_specs=row(d),
        out_shape=jax.ShapeDtypeStruct((b, t, d), F32),
        compiler_params=_params("arbitrary", "arbitrary"),
        name="moe_out",
    )(act, wd, x1, gate2, ln_g.reshape(1, d), ln_b.reshape(1, d))


def _trunk_layer(x, mod, hist, cache, w, dh, alpha):
    b, t, d = x.shape
    m = b * t
    mods = [mod[:, i:i + 1] for i in range(6)]
    shift1, scale1, gate1, shift2, scale2, gate2 = mods
    h1 = _ln_mod(x, shift1, scale1).reshape(m, d)
    (a_glu,) = _proj(h1, (w["cv"], w["cg"]), (F32,), glu=True)
    (q,) = _proj(h1, (w["q"],), (BF16,), scale=dh ** -0.5 * LOG2E)
    k, k_bf = _proj(h1, (w["k"],), (F32, BF16))
    v, v_bf = _proj(h1, (w["v"],), (F32, BF16))
    c = a_glu.shape[1]
    hd = q.shape[1]
    a_glu = a_glu.reshape(b, t, c)
    conv_out = _conv(a_glu, hist, w["conv_w"], w["conv_b"], w["conv_ln_g"], w["conv_ln_b"])
    q3, k3, v3 = (z.reshape(b, t, hd) for z in (q, k_bf, v_bf))
    if cache is None:
        attn = _attn_self(q3, k3, v3, w["tri"], dh)
    else:
        attn = _attn_cached(q3, k3, v3, cache[0], cache[1], w["tri"], dh, cache[2])
    x1, h2, gates = _out_proj(conv_out, attn, w["out"], x, gate1, scale2, shift2, w["ln1_g"],
                              w["ln1_b"], w["route_w"], w["route_b"], alpha, w["n_groups"], w["n_exp"])
    act = _moe_act(h2.reshape(m, d), gates.reshape(m, LANES), w["exp_gate"], w["exp_up"], w["n_exp"], w["f"])
    y = _moe_out(act.reshape(b, t, -1), w["exp_down"], x1, gate2, w["ln2_g"], w["ln2_b"], alpha)
    n_hist = w["conv_w"].shape[0] - 1
    assert t >= n_hist
    conv_state = a_glu[:, t - n_hist:]
    n_heads = hd // dh
    return y, conv_state, k.reshape(b, t, n_heads, dh), v.reshape(b, t, n_heads, dh)


def kernel(x_prompt, x_sample, cache_k, cache_v, state_conv, c_prompt, c_sample, w_ada, b_ada, w_in, conv_w, conv_b, conv_ln_g, conv_ln_b, w_out, ln1_g, ln1_b, w_route_group, b_route_group, w_route_expert, b_route_expert, w_exp_gate, w_exp_up, w_exp_down, ln2_g, ln2_b):
    depth = w_ada.shape[0]
    d = x_prompt.shape[2]
    bp = x_prompt.shape[0]
    bs = x_sample.shape[0]
    dh = cache_k.shape[4]
    hd = cache_k.shape[3] * dh
    c = conv_w.shape[2]
    n_groups, n_exp, _, f = w_exp_gate.shape[1:]
    alpha = (2.0 * depth) ** 0.25
    n_hist = conv_w.shape[1] - 1
    tri = -jnp.tril(jnp.ones((ATT_BLOCK, ATT_BLOCK), BF16))
    tri = jnp.concatenate([tri, tri], axis=0)
    cache_kf = cache_k.reshape(depth * bs, -1, dh)
    cache_vf = cache_v.reshape(depth * bs, -1, dh)

    xp, xs = x_prompt, x_sample
    outs = [[] for _ in range(6)]
    for l in range(depth):
        w_in_bf = w_in[l].astype(BF16)
        n_route = n_groups * (1 + n_exp)
        route_w = jnp.concatenate([w_route_group[l], w_route_expert[l], jnp.zeros((d, LANES - n_route), F32)], axis=1)
        route_b = jnp.concatenate([b_route_group[l], b_route_expert[l], jnp.zeros((LANES - n_route,), F32)])
        w = {
            "cv": w_in_bf[:, :c], "cg": w_in_bf[:, c:2 * c], "q": w_in_bf[:, 2 * c:2 * c + hd],
            "k": w_in_bf[:, 2 * c + hd:2 * c + 2 * hd], "v": w_in_bf[:, 2 * c + 2 * hd:],
            "conv_w": conv_w[l], "conv_b": conv_b[l], "conv_ln_g": conv_ln_g[l], "conv_ln_b": conv_ln_b[l],
            "out": w_out[l].astype(BF16),
            "ln1_g": ln1_g[l], "ln1_b": ln1_b[l], "ln2_g": ln2_g[l], "ln2_b": ln2_b[l],
            "route_w": route_w, "route_b": route_b.reshape(1, LANES),
            "exp_gate": w_exp_gate[l].astype(BF16).transpose(2, 0, 1, 3).reshape(d, n_groups * n_exp * f),
            "exp_up": w_exp_up[l].astype(BF16).transpose(2, 0, 1, 3).reshape(d, n_groups * n_exp * f),
            "exp_down": w_exp_down[l].astype(BF16).reshape(n_groups * n_exp * f, d),
            "tri": tri, "n_groups": n_groups, "n_exp": n_exp, "f": f,
        }
        rows = bp + bs
        pad = (-rows) % 16
        cond = jnp.concatenate([c_prompt, c_sample, jnp.zeros((pad, d), F32)], axis=0)
        mod = _ada(cond, w_ada[l], b_ada[l])[:rows].reshape(rows, 6, d)
        t_s = xs.shape[1]
        hist_p = jnp.zeros((bp, CONV_HALO, c), F32)
        hist_s = jnp.concatenate([jnp.zeros((bs, CONV_HALO - n_hist, c), F32), state_conv[l]], axis=1)
        cache = (cache_kf, cache_vf, l * bs)
        xp, cp, kp, vp = _trunk_layer(xp, mod[:bp], hist_p, None, w, dh, alpha)
        xs, cs, ks, vs = _trunk_layer(xs, mod[bp:], hist_s, cache, w, dh, alpha)
        for lst, val in zip(outs, (kp, vp, cp, ks, vs, cs)):
            lst.append(val)
    return (xp, xs) + tuple(jnp.stack(o) for o in outs)
```
